```python
import jax, jax.numpy as jnp
from jax import lax
import numpy as np

D_MODEL = 4096
BATCH = 4
SEQ = 2048
DEPTH = 2
DEC_BATCH = 8
DEC_SEQ = 1
PAST_LEN = 16384
PAGE_SIZE = 128

N_A_LAYERS = DEPTH // 2
N_B_LAYERS = DEPTH - N_A_LAYERS
N_DENSE = (DEPTH + 1) // 2
N_MOE = DEPTH // 2

DEEPNORM_ALPHA = (2.0 * DEPTH) ** 0.25
DEEPNORM_BETA = (8.0 * DEPTH) ** -0.25
LN_EPS = 1e-5

RWKV_HEAD = 64
RWKV_HEADS = D_MODEL // RWKV_HEAD
LORA_DECAY = max(32, int(round(1.8 * D_MODEL ** 0.5 / 32)) * 32)
LORA_ICLR = max(32, int(round(1.8 * D_MODEL ** 0.5 / 32)) * 32)
LORA_GATE = max(32, int(round(0.6 * D_MODEL ** 0.8 / 32)) * 32)
RWKV_LNX_EPS = 1e-5 * RWKV_HEAD
N_SHIFT_MIX = 6

ATT_HEADS = D_MODEL // 256
ATT_HEAD_DIM = 128
DILATED_GROUPS = ((128, 1), (512, 4), (2048, 16))
N_GROUPS = len(DILATED_GROUPS)
Q_WIDTH = N_GROUPS * ATT_HEADS * ATT_HEAD_DIM
ATT_OUT_WIDTH = ATT_HEADS * ATT_HEAD_DIM
ROPE_THETA = 10000.0

FFN_DIM = ((8 * D_MODEL // 3 + 255) // 256) * 256
N_EXPERTS = 8
TOP_K = 2
EXPERT_DIM = 7 * D_MODEL // 2
MOE_BLOCK = 128

kernel_name = 'yoco_rwkv7_dilated_moe_step'


def layer_norm(x, g, b, eps=LN_EPS):
    xf = x.astype(jnp.float32)
    mu = jnp.mean(xf, -1, keepdims=True)
    var = jnp.mean(jnp.square(xf - mu), -1, keepdims=True)
    y = (xf - mu) * lax.rsqrt(var + eps) * g.astype(jnp.float32) + b.astype(jnp.float32)
    return y.astype(x.dtype)


def rope(x, pos):
    hd = x.shape[-1]
    inv = ROPE_THETA ** (-jnp.arange(0, hd, 2, dtype=jnp.float32) / hd)
    ang = pos.astype(jnp.float32)[:, None] * inv[None, :]
    ang = jnp.concatenate([ang, ang], -1)
    shape = (pos.shape[0],) + (1,) * (x.ndim - 3) + (hd,)
    cos = jnp.cos(ang).reshape(shape)
    sin = jnp.sin(ang).reshape(shape)
    xf = x.astype(jnp.float32)
    x1, x2 = jnp.split(xf, 2, axis=-1)
    rot = jnp.concatenate([-x2, x1], -1)
    return (xf * cos + rot * sin).astype(x.dtype)


def wkv7_scan(state0, r, decay, k, v, a_vec, b_vec):
    xs = tuple(jnp.moveaxis(t, 1, 0) for t in (r, decay, k, v, a_vec, b_vec))

    def step(S, inp):
        r_t, w_t, k_t, v_t, a_t, b_t = inp
        sa = jnp.einsum('bhij,bhj->bhi', S, a_t)
        S = S * w_t[:, :, None, :] + sa[..., None] * b_t[:, :, None, :] + v_t[..., None] * k_t[:, :, None, :]
        return S, jnp.einsum('bhij,bhj->bhi', S, r_t)

    s_fin, ys = lax.scan(step, state0, xs)
    return jnp.moveaxis(ys, 0, 1), s_fin


def rwkv7_time_mix(x, shift_prev, wkv_prev, mix, w_rkv, w0, w1, w2, a0, a1, a2, g1, g2,
                   k_k, k_a, r_k, lnx_g, lnx_b, w_out):
    B, T, D = x.shape
    H, N = RWKV_HEADS, RWKV_HEAD
    f32 = jnp.float32
    x_prev = jnp.concatenate([shift_prev[:, None, :].astype(x.dtype), x[:, :-1]], axis=1)
    xx = x_prev - x
    xs = x[None] + xx[None] * mix[:, None, None, :]
    r, k, v = jnp.einsum('cbtd,cde->cbte', xs[:3], w_rkv)
    xw, xa, xg = xs[3], xs[4], xs[5]
    w_log = -jax.nn.softplus(-(w0 + jnp.tanh(xw @ w1) @ w2).astype(f32)) - 0.5
    decay = jnp.exp(-jnp.exp(w_log))
    a = jax.nn.sigmoid((a0 + (xa @ a1) @ a2).astype(f32))
    g = jax.nn.sigmoid(xg @ g1) @ g2

    def heads(t):
        return t.astype(f32).reshape(B, T, H, N)

    kk = heads(k * k_k)
    kk = kk / jnp.maximum(jnp.linalg.norm(kk, axis=-1, keepdims=True), 1e-12)
    k = k.astype(f32) * (1.0 + (a - 1.0) * k_a.astype(f32))
    r_h, k_h, v_h, a_h, w_h = heads(r), heads(k), heads(v), heads(a), heads(decay)
    y, wkv_new = wkv7_scan(wkv_prev.astype(f32), r_h, w_h, k_h, v_h, -kk, kk * a_h)
    mu = jnp.mean(y, -1, keepdims=True)
    var = jnp.mean(jnp.square(y - mu), -1, keepdims=True)
    y = ((y - mu) * lax.rsqrt(var + RWKV_LNX_EPS)).reshape(B, T, D) * lnx_g.astype(f32) + lnx_b.astype(f32)
    bonus = jnp.sum(r_h * k_h * r_k.astype(f32), -1, keepdims=True) * v_h
    y = (y + bonus.reshape(B, T, D)) * g.astype(f32)
    out = y.astype(x.dtype) @ w_out
    return out, x[:, -1], wkv_new.astype(wkv_prev.dtype)


def _band_dilated_prompt(q, k, v, dil, span):
    B, S, H, E = q.shape
    unit = dil * span
    s_pad = -(-S // unit) * unit
    m_len = s_pad // dil
    nblk = m_len // span

    def to_blocks(t):
        t = jnp.pad(t, ((0, 0), (0, s_pad - S), (0, 0), (0, 0)))
        t = t.reshape(B, m_len, dil, H, E).transpose(0, 2, 1, 3, 4)
        return t.reshape(B, dil, nblk, span, H, E)

    def with_prev(t):
        prev = jnp.pad(t[:, :, :-1], ((0, 0), (0, 0), (1, 0), (0, 0), (0, 0), (0, 0)))
        return jnp.concatenate([prev, t], axis=3)

    qb = to_blocks(q)
    kb = with_prev(to_blocks(k))
    vb = with_prev(to_blocks(v))
    s = jnp.einsum('brnqhe,brnkhe->brnhqk', qb, kb, preferred_element_type=jnp.float32) * (E ** -0.5)
    qi = jnp.arange(span)[:, None]
    ki = jnp.arange(2 * span)[None, :]
    dist = span + qi - ki
    key_m = jnp.arange(nblk)[:, None, None] * span - span + ki[None]
    mask = (dist >= 0) & (dist <= span) & (key_m >= 0)
    s = jnp.where(mask[None, None, :, None], s, -jnp.inf)
    m = jnp.max(s, -1, keepdims=True)
    p = jnp.exp(s - m)
    l = jnp.sum(p, -1, keepdims=True)
    o = jnp.einsum('brnhqk,brnkhe->brnhqe', p, vb.astype(jnp.float32)) / l
    lse = (m + jnp.log(l))[..., 0]
    o = o.transpose(0, 1, 2, 4, 3, 5).reshape(B, dil, m_len, H, E).transpose(0, 2, 1, 3, 4)
    o = o.reshape(B, s_pad, H, E)[:, :S]
    lse = lse.transpose(0, 1, 2, 4, 3).reshape(B, dil, m_len, H).transpose(0, 2, 1, 3)
    lse = lse.reshape(B, s_pad, H)[:, :S]
    return o, lse


def _dilated_sample(q, k_all, v_all, dil, span):
    B, DS, H, E = q.shape
    L = k_all.shape[1]
    q_idx = L - DS + jnp.arange(DS)
    idx = q_idx[:, None] - dil * jnp.arange(span + 1)[None, :]
    valid = idx >= 0
    idx = jnp.maximum(idx, 0)
    kg = k_all[:, idx]
    vg = v_all[:, idx]
    s = jnp.einsum('bjhe,bjkhe->bhjk', q, kg, preferred_element_type=jnp.float32) * (E ** -0.5)
    s = jnp.where(valid[None, None], s, -jnp.inf)
    m = jnp.max(s, -1, keepdims=True)
    p = jnp.exp(s - m)
    l = jnp.sum(p, -1, keepdims=True)
    o = jnp.einsum('bhjk,bjkhe->bhje', p, vg.astype(jnp.float32)) / l
    lse = (m + jnp.log(l))[..., 0]
    return o.transpose(0, 2, 1, 3), lse.transpose(0, 2, 1)


def shared_kv(h, pos, w_kv):
    B, T, _ = h.shape
    kv = jnp.einsum('btd,de->bte', h, w_kv).reshape(B, T, 2, N_GROUPS, ATT_HEADS, ATT_HEAD_DIM)
    return rope(kv[:, :, 0], pos), kv[:, :, 1]


def dilated_attention(x, pos, k_sh, v_sh, w_q, w_out, kv_past):
    B, T, _ = x.shape
    q = jnp.einsum('btd,de->bte', x, w_q).reshape(B, T, N_GROUPS, ATT_HEADS, ATT_HEAD_DIM)
    q = rope(q, pos)
    outs, lses = [], []
    for gi, (win, dil) in enumerate(DILATED_GROUPS):
        span = win // dil
        if kv_past is None:
            o, lse = _band_dilated_prompt(q[:, :, gi], k_sh[:, :, gi], v_sh[:, :, gi], dil, span)
        else:
            k_all = jnp.concatenate([kv_past[gi][:, :, 0].astype(k_sh.dtype), k_sh[:, :, gi]], axis=1)
            v_all = jnp.concatenate([kv_past[gi][:, :, 1].astype(v_sh.dtype), v_sh[:, :, gi]], axis=1)
            o, lse = _dilated_sample(q[:, :, gi], k_all, v_all, dil, span)
        outs.append(o)
        lses.append(lse)
    wts = jax.nn.softmax(jnp.stack(lses), axis=0)
    o = jnp.sum(wts[..., None] * jnp.stack(outs), axis=0)
    return o.reshape(B, T, ATT_OUT_WIDTH).astype(x.dtype) @ w_out


def swiglu(x, w_in, w_out):
    gate, up = jnp.split(x @ w_in, 2, axis=-1)
    return (jax.nn.silu(gate) * up) @ w_out


def moe_swiglu(x, router, w_in, w_out):
    B, T, D = x.shape
    n_tok = B * T
    xt = x.reshape(n_tok, D)
    logits = xt.astype(jnp.float32) @ router.astype(jnp.float32)
    top_v, top_i = lax.top_k(logits, TOP_K)
    gates = jax.nn.softmax(top_v, axis=-1)
    n_asg = n_tok * TOP_K
    e = top_i.reshape(n_asg)
    tok = jnp.repeat(jnp.arange(n_tok), TOP_K)
    gate = gates.reshape(n_asg)
    order = jnp.argsort(e)
    e_s, tok_s, gate_s = e[order], tok[order], gate[order]
    counts = jnp.bincount(e, length=N_EXPERTS)
    starts = jnp.cumsum(counts) - counts
    padded = -(-counts // MOE_BLOCK) * MOE_BLOCK
    padded_ends = jnp.cumsum(padded)
    padded_starts = padded_ends - padded
    dest = padded_starts[e_s] + jnp.arange(n_asg) - starts[e_s]
    n_blk = -(-n_asg // MOE_BLOCK) + N_EXPERTS
    buf = jnp.zeros((n_blk * MOE_BLOCK, D), x.dtype).at[dest].set(xt[tok_s])
    blk_e = jnp.minimum(jnp.searchsorted(padded_ends, jnp.arange(n_blk) * MOE_BLOCK, side='right'),
                        N_EXPERTS - 1)

    def expert_block(args):
        xb, ei = args
        return swiglu(xb, w_in[ei], w_out[ei])

    y_buf = lax.map(expert_block, (buf.reshape(n_blk, MOE_BLOCK, D), blk_e)).reshape(n_blk * MOE_BLOCK, D)
    y = y_buf[dest] * gate_s[:, None].astype(x.dtype)
    return jax.ops.segment_sum(y, tok_s, num_segments=n_tok).reshape(B, T, D)


def trunk(x, pos, shift0, wkv0, kv_past, p):
    new_shift, new_wkv = [], []
    k_sh = v_sh = None
    for l in range(DEPTH):
        if l < N_A_LAYERS:
            a = l
            mix, sh, st = rwkv7_time_mix(
                x, shift0[a], wkv0[a], p['rwkv_mix'][a], p['rwkv_w_rkv'][a], p['rwkv_w0'][a],
                p['rwkv_w1'][a], p['rwkv_w2'][a], p['rwkv_a0'][a], p['rwkv_a1'][a], p['rwkv_a2'][a],
                p['rwkv_g1'][a], p['rwkv_g2'][a], p['rwkv_k_k'][a], p['rwkv_k_a'][a], p['rwkv_r_k'][a],
                p['rwkv_lnx_g'][a], p['rwkv_lnx_b'][a], p['rwkv_w_out'][a])
            new_shift.append(sh)
            new_wkv.append(st)
        else:
            b = l - N_A_LAYERS
            mix = dilated_attention(x, pos, k_sh, v_sh, p['attn_w_q'][b], p['attn_w_out'][b], kv_past)
        x = layer_norm(DEEPNORM_ALPHA * x + mix, p['ln_g'][l, 0], p['ln_b'][l, 0])
        if l % 2 == 0:
            f = swiglu(x, p['ffn_w_in'][l // 2], p['ffn_w_out'][l // 2])
        else:
            f = moe_swiglu(x, p['moe_router'][l // 2], p['moe_w_in'][l // 2], p['moe_w_out'][l // 2])
        x = layer_norm(DEEPNORM_ALPHA * x + f, p['ln_g'][l, 1], p['ln_b'][l, 1])
        if l == N_A_LAYERS - 1:
            k_sh, v_sh = shared_kv(x, pos, p['attn_w_kv'])
    T = x.shape[1]
    kv_rows = []
    for gi, (win, dil) in enumerate(DILATED_GROUPS):
        keep = T if kv_past is not None else min(win, T)
        kv_rows.append(jnp.stack([k_sh[:, T - keep:, gi], v_sh[:, T - keep:, gi]], axis=2))
    return x, jnp.stack(new_shift), jnp.stack(new_wkv), kv_rows


def setup_inputs(seed: int = 0) -> dict:
    key = jax.random.key(seed)
    ks = iter(jax.random.split(key, 48))
    f32 = jnp.float32

    def nrm(shape, scale=1.0):
        return jax.random.normal(next(ks), shape, f32) * scale

    def unif(shape, lo, hi):
        return jax.random.uniform(next(ks), shape, f32, lo, hi)

    D, H, N, na = D_MODEL, RWKV_HEADS, RWKV_HEAD, N_A_LAYERS
    d = {}
    d['x_prompt'] = nrm((BATCH, SEQ, D))
    d['x_sample'] = nrm((DEC_BATCH, DEC_SEQ, D))
    d['state_wkv'] = nrm((na, DEC_BATCH, H, N, N), 0.1)
    d['state_shift'] = nrm((na, DEC_BATCH, D))
    d['cache_kv_w128'] = nrm((DEC_BATCH, min(DILATED_GROUPS[0][0], PAST_LEN), 2, ATT_HEADS, ATT_HEAD_DIM))
    d['cache_kv_w512'] = nrm((DEC_BATCH, min(DILATED_GROUPS[1][0], PAST_LEN), 2, ATT_HEADS, ATT_HEAD_DIM))
    d['cache_kv_w2048'] = nrm((DEC_BATCH, min(DILATED_GROUPS[2][0], PAST_LEN), 2, ATT_HEADS, ATT_HEAD_DIM))
    d['ln_g'] = 1.0 + nrm((DEPTH, 2, D), 0.02)
    d['ln_b'] = nrm((DEPTH, 2, D), 0.02)
    d['rwkv_mix'] = unif((na, N_SHIFT_MIX, D), 0.0, 1.0)
    d['rwkv_w_rkv'] = nrm((na, 3, D, D), D ** -0.5)
    d['rwkv_w0'] = unif((na, D), -5.0, 0.5)
    d['rwkv_w1'] = nrm((na, D, LORA_DECAY), D ** -0.5)
    d['rwkv_w2'] = nrm((na, LORA_DECAY, D), 0.1 * LORA_DECAY ** -0.5)
    d['rwkv_a0'] = nrm((na, D), 0.5)
    d['rwkv_a1'] = nrm((na, D, LORA_ICLR), D ** -0.5)
    d['rwkv_a2'] = nrm((na, LORA_ICLR, D), 0.1 * LORA_ICLR ** -0.5)
    d['rwkv_g1'] = nrm((na, D, LORA_GATE), D ** -0.5)
    d['rwkv_g2'] = nrm((na, LORA_GATE, D), LORA_GATE ** -0.5)
    d['rwkv_k_k'] = 0.85 + nrm((na, D), 0.02)
    d['rwkv_k_a'] = 1.0 + nrm((na, D), 0.02)
    d['rwkv_r_k'] = nrm((na, H, N), 0.1)
    d['rwkv_lnx_g'] = 1.0 + nrm((na, D), 0.02)
    d['rwkv_lnx_b'] = nrm((na, D), 0.02)
    d['rwkv_w_out'] = nrm((na, D, D), DEEPNORM_BETA * D ** -0.5)
    d['attn_w_kv'] = nrm((D, 2 * Q_WIDTH), D ** -0.5)
    d['attn_w_q'] = nrm((N_B_LAYERS, D, Q_WIDTH), D ** -0.5)
    d['attn_w_out'] = nrm((N_B_LAYERS, ATT_OUT_WIDTH, D), DEEPNORM_BETA * ATT_OUT_WIDTH ** -0.5)
    d['ffn_w_in'] = nrm((N_DENSE, D, 2 * FFN_DIM), D ** -0.5)
    d['ffn_w_out'] = nrm((N_DENSE, FFN_DIM, D), DEEPNORM_BETA * FFN_DIM ** -0.5)
    d['moe_router'] = nrm((N_MOE, D, N_EXPERTS), D ** -0.5)
    d['moe_w_in'] = nrm((N_MOE, N_EXPERTS, D, 2 * EXPERT_DIM), D ** -0.5)
    d['moe_w_out'] = nrm((N_MOE, N_EXPERTS, EXPERT_DIM, D), DEEPNORM_BETA * EXPERT_DIM ** -0.5)
    return d


def reference(x_prompt, x_sample, state_wkv, state_shift, cache_kv_w128, cache_kv_w512, cache_kv_w2048,
              ln_g, ln_b, rwkv_mix, rwkv_w_rkv, rwkv_w0, rwkv_w1, rwkv_w2, rwkv_a0, rwkv_a1, rwkv_a2,
              rwkv_g1, rwkv_g2, rwkv_k_k, rwkv_k_a, rwkv_r_k, rwkv_lnx_g, rwkv_lnx_b, rwkv_w_out,
              attn_w_kv, attn_w_q, attn_w_out, ffn_w_in, ffn_w_out, moe_router, moe_w_in, moe_w_out):
    p = dict(ln_g=ln_g, ln_b=ln_b, rwkv_mix=rwkv_mix, rwkv_w_rkv=rwkv_w_rkv, rwkv_w0=rwkv_w0,
             rwkv_w1=rwkv_w1, rwkv_w2=rwkv_w2, rwkv_a0=rwkv_a0, rwkv_a1=rwkv_a1, rwkv_a2=rwkv_a2,
             rwkv_g1=rwkv_g1, rwkv_g2=rwkv_g2, rwkv_k_k=rwkv_k_k, rwkv_k_a=rwkv_k_a, rwkv_r_k=rwkv_r_k,
             rwkv_lnx_g=rwkv_lnx_g, rwkv_lnx_b=rwkv_lnx_b, rwkv_w_out=rwkv_w_out, attn_w_kv=attn_w_kv,
             attn_w_q=attn_w_q, attn_w_out=attn_w_out, ffn_w_in=ffn_w_in, ffn_w_out=ffn_w_out,
             moe_router=moe_router, moe_w_in=moe_w_in, moe_w_out=moe_w_out)
    b_p, t_p = x_prompt.shape[0], x_prompt.shape[1]
    pos_p = jnp.arange(t_p, dtype=jnp.int32)
    shift0 = jnp.zeros((N_A_LAYERS, b_p, D_MODEL), x_prompt.dtype)
    wkv0 = jnp.zeros((N_A_LAYERS, b_p, RWKV_HEADS, RWKV_HEAD, RWKV_HEAD), state_wkv.dtype)
    y_prompt, prompt_shift, prompt_wkv, prompt_kv = trunk(x_prompt, pos_p, shift0, wkv0, None, p)
    pos_s = PAST_LEN + jnp.arange(x_sample.shape[1], dtype=jnp.int32)
    y_sample, sample_shift, sample_wkv, sample_kv = trunk(
        x_sample, pos_s, state_shift, state_wkv, (cache_kv_w128, cache_kv_w512, cache_kv_w2048), p)
    prompt_kv_w128, prompt_kv_w512, prompt_kv_w2048 = prompt_kv
    sample_kv_w128, sample_kv_w512, sample_kv_w2048 = sample_kv
    return (y_prompt, y_sample, prompt_wkv, prompt_shift, prompt_kv_w128, prompt_kv_w512, prompt_kv_w2048,
            sample_wkv, sample_shift, sample_kv_w128, sample_kv_w512, sample_kv_w2048)
```

```python
import functools

import jax
import jax.numpy as jnp
from jax import lax
from jax.experimental import pallas as pl
from jax.experimental.pallas import tpu as pltpu

F32 = jnp.float32
BF16 = jnp.bfloat16

LANES = 128
SUBLANES = 8
VMEM_LIMIT = 56 * 1024 * 1024

LN_EPS = 1e-5
ROPE_THETA = 10000.0
PAST_LEN = 16384
MOE_ROWS = 512


def _cparams(sem):
    return pltpu.CompilerParams(dimension_semantics=sem, vmem_limit_bytes=VMEM_LIMIT)


def _mm_body(x_ref, w_ref, *rest, nk, epilogue, has_bias):
    if has_bias:
        b_ref, o_ref, acc_ref = rest
    else:
        o_ref, acc_ref = rest
    k = pl.program_id(2)

    @pl.when(k == 0)
    def _init():
        acc_ref[...] = jnp.zeros_like(acc_ref)

    acc_ref[...] += jnp.dot(x_ref[...].astype(BF16), w_ref[...].astype(BF16),
                            preferred_element_type=F32)

    @pl.when(k == nk - 1)
    def _fin():
        y = acc_ref[...]
        if has_bias:
            y = y + b_ref[...]
        if epilogue is not None:
            y = epilogue(y)
        o_ref[...] = y.astype(o_ref.dtype)


def _matmul(x, w3, li, *, tm, tn, tk, out_dtype=F32, bias=None, epilogue=None, name="mm"):
    m, kdim = x.shape
    _, kw, n = w3.shape
    assert kw == kdim and m % tm == 0 and n % tn == 0 and kdim % tk == 0
    nk = kdim // tk
    in_specs = [pl.BlockSpec((tm, tk), lambda i, j, k: (i, k)),
                pl.BlockSpec((None, tk, tn), lambda i, j, k: (li, k, j))]
    args = [x, w3]
    if bias is not None:
        in_specs.append(pl.BlockSpec((1, tn), lambda i, j, k: (0, j)))
        args.append(bias.reshape(1, n).astype(F32))
    return pl.pallas_call(
        functools.partial(_mm_body, nk=nk, epilogue=epilogue, has_bias=bias is not None),
        grid=(m // tm, n // tn, nk),
        in_specs=in_specs,
        out_specs=pl.BlockSpec((tm, tn), lambda i, j, k: (i, j)),
        out_shape=jax.ShapeDtypeStruct((m, n), out_dtype),
        scratch_shapes=[pltpu.VMEM((tm, tn), F32)],
        compiler_params=_cparams(("parallel", "parallel", "arbitrary")),
        name=name,
    )(*args)


def _swiglu_body(x_ref, wg_ref, wu_ref, o_ref, accg_ref, accu_ref, *, nk):
    k = pl.program_id(2)

    @pl.when(k == 0)
    def _init():
        accg_ref[...] = jnp.zeros_like(accg_ref)
        accu_ref[...] = jnp.zeros_like(accu_ref)

    xb = x_ref[...].astype(BF16)
    accg_ref[...] += jnp.dot(xb, wg_ref[...].astype(BF16), preferred_element_type=F32)
    accu_ref[...] += jnp.dot(xb, wu_ref[...].astype(BF16), preferred_element_type=F32)

    @pl.when(k == nk - 1)
    def _fin():
        g = accg_ref[...]
        o_ref[...] = (g * jax.nn.sigmoid(g) * accu_ref[...]).astype(o_ref.dtype)


def _swiglu_in(x, w3, li, *, tm, tn, tk, name="swiglu_in"):
    m, kdim = x.shape
    _, _, n2 = w3.shape
    hid = n2 // 2
    assert m % tm == 0 and hid % tn == 0 and kdim % tk == 0
    nk = kdim // tk
    nt = hid // tn
    return pl.pallas_call(
        functools.partial(_swiglu_body, nk=nk),
        grid=(m // tm, nt, nk),
        in_specs=[pl.BlockSpec((tm, tk), lambda i, j, k: (i, k)),
                  pl.BlockSpec((None, tk, tn), lambda i, j, k: (li, k, j)),
                  pl.BlockSpec((None, tk, tn), lambda i, j, k: (li, k, j + nt))],
        out_specs=pl.BlockSpec((tm, tn), lambda i, j, k: (i, j)),
        out_shape=jax.ShapeDtypeStruct((m, hid), BF16),
        scratch_shapes=[pltpu.VMEM((tm, tn), F32), pltpu.VMEM((tm, tn), F32)],
        compiler_params=_cparams(("parallel", "parallel", "arbitrary")),
        name=name,
    )(x, w3, w3)


def _ln_body(x_ref, f_ref, g_ref, b_ref, o_ref, ob_ref, *, alpha):
    z = alpha * x_ref[...] + f_ref[...]
    mu = jnp.mean(z, -1, keepdims=True)
    zc = z - mu
    var = jnp.mean(zc * zc, -1, keepdims=True)
    y = zc * lax.rsqrt(var + LN_EPS) * g_ref[...] + b_ref[...]
    o_ref[...] = y
    ob_ref[...] = y.astype(BF16)


def _res_ln(x, f, g, b, *, alpha, tm, name="res_ln"):
    m, d = x.shape
    assert m % tm == 0
    row = pl.BlockSpec((tm, d), lambda i: (i, 0))
    vec = pl.BlockSpec((1, d), lambda i: (0, 0))
    return pl.pallas_call(
        functools.partial(_ln_body, alpha=alpha),
        grid=(m // tm,),
        in_specs=[row, row, vec, vec],
        out_specs=[row, row],
        out_shape=[jax.ShapeDtypeStruct((m, d), F32), jax.ShapeDtypeStruct((m, d), BF16)],
        compiler_params=_cparams(("parallel",)),
        name=name,
    )(x, f, g.reshape(1, d), b.reshape(1, d))


def _wkv_body(r_ref, w_ref, k_ref, v_ref, a_ref, b_ref, s0_ref, y_ref, s_ref, *, tc, n):
    t_blk = pl.program_id(1)

    @pl.when(t_blk == 0)
    def _init():
        s_ref[...] = s0_ref[...]

    def row(ref, t, j):
        return ref[t, pl.ds(j, 1), :]

    def step(t, carry):
        sa = jnp.zeros((n, LANES), F32)
        for j in range(n):
            sa = sa + s_ref[j] * row(a_ref, t, j)
        v_t = v_ref[t]
        y = jnp.zeros((n, LANES), F32)
        for j in range(n):
            s_new = s_ref[j] * row(w_ref, t, j) + sa * row(b_ref, t, j) + v_t * row(k_ref, t, j)
            s_ref[j] = s_new
            y = y + s_new * row(r_ref, t, j)
        y_ref[t] = y
        return carry

    lax.fori_loop(0, tc, step, 0)


def _wkv_scan(r, w, k, v, a, b, s0, *, tc, name="wkv_scan"):
    t_len, n, lanes = r.shape
    assert lanes % LANES == 0 and t_len % tc == 0
    seq = pl.BlockSpec((tc, n, LANES), lambda l, t: (t, 0, l))
    st = pl.BlockSpec((n, n, LANES), lambda l, t: (0, 0, l))
    return pl.pallas_call(
        functools.partial(_wkv_body, tc=tc, n=n),
        grid=(lanes // LANES, t_len // tc),
        in_specs=[seq] * 6 + [st],
        out_specs=[seq, st],
        out_shape=[jax.ShapeDtypeStruct((t_len, n, lanes), F32),
                   jax.ShapeDtypeStruct((n, n, lanes), F32)],
        compiler_params=_cparams(("parallel", "arbitrary")),
        name=name,
    )(r, w, k, v, a, b, s0)


SPAN = 128


def _attn_block(q, k, v, causal_own, scale):
    s = lax.dot_general(q.astype(BF16), k.astype(BF16), (((1,), (1,)), ((), ())),
                        preferred_element_type=F32) * scale
    qi = lax.broadcasted_iota(jnp.int32, s.shape, 0)
    ki = lax.broadcasted_iota(jnp.int32, s.shape, 1)
    if causal_own:
        mask = ki <= qi
    else:
        mask = (ki >= qi) & (ki <= qi + SPAN)
    s = jnp.where(mask, s, -jnp.inf)
    m = jnp.max(s, axis=1, keepdims=True)
    p = jnp.exp(s - m)
    l = jnp.sum(p, axis=1, keepdims=True)
    o = jnp.dot(p.astype(BF16), v.astype(BF16), preferred_element_type=F32)
    return o, m, l


def _dilated_attn_body(*refs, dils, seq, scale):
    ng = len(dils)
    q_refs, k_refs, v_refs = refs[:ng], refs[ng:2 * ng], refs[2 * ng:3 * ng]
    o_ref = refs[3 * ng]
    scratch = refs[3 * ng + 1:]
    acc_refs, m_refs, l_refs = scratch[:ng], scratch[ng:2 * ng], scratch[2 * ng:]

    for g, dil in enumerate(dils):
        q_ref, k_ref, v_ref = q_refs[g], k_refs[g], v_refs[g]
        acc_ref, m_ref, l_ref = acc_refs[g], m_refs[g], l_refs[g]
        nblk = seq // (dil * SPAN)

        def rows(start, size, dil=dil):
            if dil == 1:
                return pl.ds(start, size)
            return pl.ds(start, size, stride=dil)

        def put(sl, res, acc_ref=acc_ref, m_ref=m_ref, l_ref=l_ref):
            o, m, l = res
            acc_ref[sl, :] = o
            m_ref[sl, :] = m
            l_ref[sl, :] = l

        for r in range(dil):
            sl0 = rows(r, SPAN)
            put(sl0, _attn_block(q_ref[sl0, :], k_ref[sl0, :], v_ref[sl0, :], True, scale))

            def body(nb, carry, r=r, dil=dil, rows=rows, put=put, q_ref=q_ref, k_ref=k_ref, v_ref=v_ref):
                q_sl = rows(r + dil * SPAN * nb, SPAN)
                kv_sl = rows(r + dil * SPAN * (nb - 1), 2 * SPAN)
                put(q_sl, _attn_block(q_ref[q_sl, :], k_ref[kv_sl, :], v_ref[kv_sl, :], False, scale))
                return carry

            if nblk > 1:
                lax.fori_loop(1, nblk, body, 0)

    m_all = m_refs[0][...]
    for g in range(1, ng):
        m_all = jnp.maximum(m_all, m_refs[g][...])
    num = jnp.zeros(acc_refs[0].shape, F32)
    den = jnp.zeros(m_all.shape, F32)
    for g in range(ng):
        e = jnp.exp(m_refs[g][...] - m_all)
        num = num + e * acc_refs[g][...]
        den = den + e * l_refs[g][...]
    o_ref[...] = (num / den).astype(o_ref.dtype)


def _dilated_attn(q_hm, k_hm, v_hm, *, batch, seq, heads, dils, name="dilated_attn"):
    e = q_hm.shape[-1]
    ng = len(dils)

    def spec(g):
        return pl.BlockSpec((None, seq, e), lambda b, h, g=g: (g * heads + h, b, 0))

    specs = [spec(g) for g in range(ng)]
    return pl.pallas_call(
        functools.partial(_dilated_attn_body, dils=dils, seq=seq, scale=e ** -0.5),
        grid=(batch, heads),
        in_specs=specs * 3,
        out_specs=pl.BlockSpec((seq, e), lambda b, h: (b, h)),
        out_shape=jax.ShapeDtypeStruct((batch * seq, heads * e), BF16),
        scratch_shapes=([pltpu.VMEM((seq, e), F32)] * ng + [pltpu.VMEM((seq, 1), F32)] * (2 * ng)),
        compiler_params=_cparams(("parallel", "parallel")),
        name=name,
    )(*([q_hm] * ng + [k_hm] * ng + [v_hm] * ng))


def _router_body(x_ref, w_ref, o_ref):
    o_ref[...] = jnp.dot(x_ref[...], w_ref[...], precision=lax.Precision.HIGHEST,
                         preferred_element_type=F32)


def _router_logits(x, router, *, tm, name="router_logits"):
    m, d = x.shape
    n_exp = router.shape[1]
    w = jnp.zeros((d, LANES), F32).at[:, :n_exp].set(router.astype(F32))
    out = pl.pallas_call(
        _router_body,
        grid=(m // tm,),
        in_specs=[pl.BlockSpec((tm, d), lambda i: (i, 0)), pl.BlockSpec((d, LANES), lambda i: (0, 0))],
        out_specs=pl.BlockSpec((tm, LANES), lambda i: (i, 0)),
        out_shape=jax.ShapeDtypeStruct((m, LANES), F32),
        compiler_params=_cparams(("parallel",)),
        name=name,
    )(x, w)
    return out[:, :n_exp]


def _moe_in_body(be_ref, nu_ref, x_ref, wg_ref, wu_ref, o_ref, wgb_ref, wub_ref):
    r = pl.program_id(1)
    changed = (r == 0) | (be_ref[r] != be_ref[jnp.maximum(r - 1, 0)])

    @pl.when(changed)
    def _cast():
        wgb_ref[...] = wg_ref[...].astype(BF16)
        wub_ref[...] = wu_ref[...].astype(BF16)

    @pl.when(r < nu_ref[0])
    def _compute():
        x = x_ref[...]
        g = jnp.dot(x, wgb_ref[...], preferred_element_type=F32)
        u = jnp.dot(x, wub_ref[...], preferred_element_type=F32)
        o_ref[...] = (g * jax.nn.sigmoid(g) * u).astype(o_ref.dtype)

    @pl.when(r >= nu_ref[0])
    def _skip():
        o_ref[...] = jnp.zeros_like(o_ref)


def _moe_in(x_buf, w_in, blk_e, n_used, *, tn, name="moe_swiglu_in"):
    rows, d = x_buf.shape
    n_exp, _, n2 = w_in.shape
    hid = n2 // 2
    nt = hid // tn
    rb = rows // MOE_ROWS

    def x_map(j, r, be, nu):
        return (jnp.minimum(r, nu[0] - 1), 0)

    grid_spec = pltpu.PrefetchScalarGridSpec(
        num_scalar_prefetch=2,
        grid=(nt, rb),
        in_specs=[pl.BlockSpec((MOE_ROWS, d), x_map),
                  pl.BlockSpec((None, d, tn), lambda j, r, be, nu: (be[r], 0, j)),
                  pl.BlockSpec((None, d, tn), lambda j, r, be, nu: (be[r], 0, j + nt))],
        out_specs=pl.BlockSpec((MOE_ROWS, tn), lambda j, r, be, nu: (r, j)),
        scratch_shapes=[pltpu.VMEM((d, tn), BF16), pltpu.VMEM((d, tn), BF16)],
    )
    return pl.pallas_call(
        _moe_in_body,
        grid_spec=grid_spec,
        out_shape=jax.ShapeDtypeStruct((rows, hid), BF16),
        compiler_params=_cparams(("arbitrary", "arbitrary")),
        name=name,
    )(blk_e, n_used, x_buf, w_in, w_in)


def _moe_out_body(be_ref, nu_ref, h_ref, w_ref, o_ref, acc_ref, *, nk):
    r = pl.program_id(0)
    k = pl.program_id(1)

    @pl.when(k == 0)
    def _init():
        acc_ref[...] = jnp.zeros_like(acc_ref)

    @pl.when(r < nu_ref[0])
    def _compute():
        acc_ref[...] += jnp.dot(h_ref[...], w_ref[...].astype(BF16), preferred_element_type=F32)

    @pl.when(k == nk - 1)
    def _fin():
        o_ref[...] = acc_ref[...]


def _moe_out(h_buf, w_out, blk_e, n_used, *, tk, name="moe_out"):
    rows, hid = h_buf.shape
    n_exp, _, d = w_out.shape
    rb = rows // MOE_ROWS
    nk = hid // tk

    def k_eff(r, k, nu):
        return jnp.where(r < nu[0], k, nk - 1)

    grid_spec = pltpu.PrefetchScalarGridSpec(
        num_scalar_prefetch=2,
        grid=(rb, nk),
        in_specs=[pl.BlockSpec((MOE_ROWS, tk),
                               lambda r, k, be, nu: (jnp.minimum(r, nu[0] - 1), k_eff(r, k, nu))),
                  pl.BlockSpec((None, tk, d), lambda r, k, be, nu: (be[r], k_eff(r, k, nu), 0))],
        out_specs=pl.BlockSpec((MOE_ROWS, d), lambda r, k, be, nu: (r, 0)),
        scratch_shapes=[pltpu.VMEM((MOE_ROWS, d), F32)],
    )
    return pl.pallas_call(
        functools.partial(_moe_out_body, nk=nk),
        grid_spec=grid_spec,
        out_shape=jax.ShapeDtypeStruct((rows, d), F32),
        compiler_params=_cparams(("arbitrary", "arbitrary")),
        name=name,
    )(blk_e, n_used, h_buf, w_out)


def _moe(x_f32, x_bf16, n_tok, router, w_in, w_out):
    rows, d = x_f32.shape
    n_exp = router.shape[1]
    top_k = 2
    logits = _router_logits(x_f32, router, tm=520)[:n_tok]
    top_v, top_i = lax.top_k(logits, top_k)
    gates = jax.nn.softmax(top_v, axis=-1)
    n_asg = n_tok * top_k
    e = top_i.reshape(n_asg).astype(jnp.int32)
    tok = jnp.repeat(jnp.arange(n_tok, dtype=jnp.int32), top_k)
    order = jnp.argsort(e)
    e_s, tok_s = e[order], tok[order]
    counts = jnp.bincount(e, length=n_exp).astype(jnp.int32)
    starts = jnp.cumsum(counts) - counts
    padded = -(-counts // MOE_ROWS) * MOE_ROWS
    padded_ends = jnp.cumsum(padded)
    padded_starts = padded_ends - padded
    dest = (padded_starts[e_s] + jnp.arange(n_asg, dtype=jnp.int32) - starts[e_s]).astype(jnp.int32)
    n_blk = -(-n_asg // MOE_ROWS) + n_exp
    src_tok = jnp.zeros((n_blk * MOE_ROWS,), jnp.int32).at[dest].set(tok_s)
    pos = jnp.zeros((n_asg,), jnp.int32).at[order].set(dest).reshape(n_tok, top_k)
    blk_e = jnp.minimum(jnp.searchsorted(padded_ends, jnp.arange(n_blk, dtype=jnp.int32) * MOE_ROWS,
                                         side='right'), n_exp - 1).astype(jnp.int32)
    n_used = (padded_ends[-1:] // MOE_ROWS).astype(jnp.int32)

    x_buf = x_bf16[src_tok]
    h_buf = _moe_in(x_buf, w_in, blk_e, n_used, tn=512)
    y_buf = _moe_out(h_buf, w_out, blk_e, n_used, tk=512)
    y = y_buf[pos[:, 0]] * gates[:, 0:1] + y_buf[pos[:, 1]] * gates[:, 1:2]
    return jnp.zeros((rows, d), F32).at[:n_tok].set(y)


def _rope_tables(pos, hd):
    inv = ROPE_THETA ** (-jnp.arange(0, hd, 2, dtype=F32) / hd)
    ang = pos.astype(F32)[:, None] * inv[None, :]
    ang = jnp.concatenate([ang, ang], -1)
    return jnp.cos(ang), jnp.sin(ang)


def _rope_rows(x, cos, sin):
    shape = (x.shape[0],) + (1,) * (x.ndim - 2) + (x.shape[-1],)
    x1, x2 = jnp.split(x, 2, axis=-1)
    rot = jnp.concatenate([-x2, x1], -1)
    return x * cos.reshape(shape) + rot * sin.reshape(shape)


def _softplus(z):
    return jnp.maximum(z, 0.0) + jnp.log(1.0 + jnp.exp(-jnp.abs(z)))


def _decay_epilogue(y):
    return jnp.exp(-jnp.exp(-_softplus(-y) - 0.5))


def _sample_attn_group(q, k_all, v_all, dil, span):
    n_new = q.shape[1]
    length = k_all.shape[1]
    q_idx = length - n_new + jnp.arange(n_new)
    idx = q_idx[:, None] - dil * jnp.arange(span + 1)[None, :]
    valid = idx >= 0
    idx = jnp.maximum(idx, 0)
    kg = k_all[:, idx]
    vg = v_all[:, idx]
    s = jnp.einsum('bjhe,bjkhe->bhjk', q, kg, preferred_element_type=F32) * (q.shape[-1] ** -0.5)
    s = jnp.where(valid[None, None], s, -jnp.inf)
    m = jnp.max(s, -1, keepdims=True)
    p = jnp.exp(s - m)
    l = jnp.sum(p, -1, keepdims=True)
    o = jnp.einsum('bhjk,bjkhe->bhje', p, vg) / l
    lse = (m + jnp.log(l))[..., 0]
    return o.transpose(0, 2, 1, 3), lse.transpose(0, 2, 1)


def kernel(x_prompt, x_sample, state_wkv, state_shift, cache_kv_w128, cache_kv_w512, cache_kv_w2048,
           ln_g, ln_b, rwkv_mix, rwkv_w_rkv, rwkv_w0, rwkv_w1, rwkv_w2, rwkv_a0, rwkv_a1, rwkv_a2,
           rwkv_g1, rwkv_g2, rwkv_k_k, rwkv_k_a, rwkv_r_k, rwkv_lnx_g, rwkv_lnx_b, rwkv_w_out,
           attn_w_kv, attn_w_q, attn_w_out, ffn_w_in, ffn_w_out, moe_router, moe_w_in, moe_w_out):
    bp, seq, d = x_prompt.shape
    bs, seq_s, _ = x_sample.shape
    depth = ln_g.shape[0]
    assert depth == 2 and seq_s == 1
    heads_r, hd_r = rwkv_r_k.shape[1], rwkv_r_k.shape[2]
    past_len = PAST_LEN
    groups = ((128, 1), (512, 4), (2048, 16))
    ng = len(groups)
    hd_a = 128
    heads_a = attn_w_out.shape[1] // hd_a
    alpha = (2.0 * depth) ** 0.25
    lnx_eps = 1e-5 * hd_r

    n_p = bp * seq
    n_tok = n_p + bs
    tm = 2080
    rows = -(-n_tok // tm) * tm

    def pad_rows(a):
        return jnp.concatenate([a, jnp.zeros((rows - a.shape[0],) + a.shape[1:], a.dtype)], 0)

    x0 = pad_rows(jnp.concatenate([x_prompt.reshape(n_p, d), x_sample.reshape(bs, d)], 0))
    prev_p = jnp.concatenate([jnp.zeros((bp, 1, d), F32), x_prompt[:, :-1]], 1).reshape(n_p, d)
    x_prev = pad_rows(jnp.concatenate([prev_p, state_shift[0]], 0))
    xx = x_prev - x0
    xs = [(x0 + xx * rwkv_mix[0, c]).astype(BF16) for c in range(6)]

    w_rkv = rwkv_w_rkv.reshape(3, d, d)
    mm = functools.partial(_matmul, tm=tm)
    r = mm(xs[0], w_rkv, 0, tn=1024, tk=512, name="rwkv_r")
    k = mm(xs[1], w_rkv, 1, tn=1024, tk=512, name="rwkv_k")
    v = mm(xs[2], w_rkv, 2, tn=1024, tk=512, name="rwkv_v")
    lw, la, lg = rwkv_w1.shape[2], rwkv_a1.shape[2], rwkv_g1.shape[2]
    hw = mm(xs[3], rwkv_w1, 0, tn=lw, tk=1024, out_dtype=BF16, epilogue=jnp.tanh, name="rwkv_w1")
    decay = mm(hw, rwkv_w2, 0, tn=1024, tk=lw, bias=rwkv_w0[0], epilogue=_decay_epilogue, name="rwkv_w2")
    ha = mm(xs[4], rwkv_a1, 0, tn=la, tk=1024, out_dtype=BF16, name="rwkv_a1")
    a = mm(ha, rwkv_a2, 0, tn=1024, tk=la, bias=rwkv_a0[0], epilogue=jax.nn.sigmoid, name="rwkv_a2")
    hg = mm(xs[5], rwkv_g1, 0, tn=lg, tk=1024, out_dtype=BF16, epilogue=jax.nn.sigmoid, name="rwkv_g1")
    g = mm(hg, rwkv_g2, 0, tn=1024, tk=lg, name="rwkv_g2")

    def heads(t):
        return t.reshape(rows, heads_r, hd_r)

    kk = heads(k * rwkv_k_k[0])
    kk = kk / jnp.maximum(jnp.linalg.norm(kk, axis=-1, keepdims=True), 1e-12)
    k2 = k * (1.0 + (a - 1.0) * rwkv_k_a[0])
    r_h, k_h, v_h, a_h, w_h = heads(r), heads(k2), heads(v), heads(a), heads(decay)
    bonus = (jnp.sum(r_h * k_h * rwkv_r_k[0], -1, keepdims=True) * v_h).reshape(rows, d)
    scan_in = (r_h, w_h, k_h, v_h, -kk, kk * a_h)

    def lanes_p(t):
        return jnp.transpose(t[:n_p].reshape(bp, seq, heads_r, hd_r), (1, 3, 0, 2)).reshape(seq, hd_r, bp * heads_r)

    def lanes_s(t):
        return jnp.transpose(t[n_p:n_tok], (2, 0, 1)).reshape(1, hd_r, bs * heads_r)

    s0_p = jnp.zeros((hd_r, hd_r, bp * heads_r), F32)
    s0_s = jnp.transpose(state_wkv[0].astype(F32), (3, 2, 0, 1)).reshape(hd_r, hd_r, bs * heads_r)
    y_p, s_p = _wkv_scan(*[lanes_p(t) for t in scan_in], s0_p, tc=32, name="wkv_scan_prompt")
    y_s, s_s = _wkv_scan(*[lanes_s(t) for t in scan_in], s0_s, tc=1, name="wkv_scan_sample")
    y_p = jnp.transpose(y_p.reshape(seq, hd_r, bp, heads_r), (2, 0, 3, 1)).reshape(n_p, heads_r, hd_r)
    y_s = jnp.transpose(y_s.reshape(hd_r, bs, heads_r), (1, 2, 0))
    y = pad_rows(jnp.concatenate([y_p, y_s], 0))
    prompt_wkv = jnp.transpose(s_p.reshape(hd_r, hd_r, bp, heads_r), (2, 3, 1, 0))[None].astype(state_wkv.dtype)
    sample_wkv = jnp.transpose(s_s.reshape(hd_r, hd_r, bs, heads_r), (2, 3, 1, 0))[None].astype(state_wkv.dtype)

    mu = jnp.mean(y, -1, keepdims=True)
    var = jnp.mean(jnp.square(y - mu), -1, keepdims=True)
    y = ((y - mu) * lax.rsqrt(var + lnx_eps)).reshape(rows, d) * rwkv_lnx_g[0] + rwkv_lnx_b[0]
    y = ((y + bonus) * g).astype(BF16)
    mix0 = mm(y, rwkv_w_out, 0, tn=1024, tk=512, name="rwkv_out")
    x1, x1b = _res_ln(x0, mix0, ln_g[0, 0], ln_b[0, 0], alpha=alpha, tm=208, name="ln_0a")

    h = _swiglu_in(x1b, ffn_w_in, 0, tm=tm, tn=256, tk=1024, name="ffn_in")
    f0 = mm(h, ffn_w_out, 0, tn=1024, tk=256, name="ffn_out")
    x2, x2b = _res_ln(x1, f0, ln_g[0, 1], ln_b[0, 1], alpha=alpha, tm=208, name="ln_0b")

    pos = jnp.concatenate([jnp.tile(jnp.arange(seq, dtype=jnp.int32), bp),
                           jnp.full((bs,), past_len, jnp.int32),
                           jnp.zeros((rows - n_tok,), jnp.int32)])
    cos, sin = _rope_tables(pos, hd_a)
    kv = mm(x2b, attn_w_kv.reshape(1, d, -1), 0, tn=1024, tk=512, name="attn_kv")
    kv = kv.reshape(rows, 2, ng, heads_a, hd_a)
    k_sh = _rope_rows(kv[:, 0], cos, sin)
    v_sh = kv[:, 1]
    q = mm(x2b, attn_w_q, 0, tn=1024, tk=512, name="attn_q")
    q = _rope_rows(q.reshape(rows, ng, heads_a, hd_a), cos, sin)

    def head_major(t):
        return jnp.transpose(t[:n_p].reshape(n_p, ng * heads_a, hd_a), (1, 0, 2))

    att_p = _dilated_attn(head_major(q), head_major(k_sh), head_major(v_sh), batch=bp, seq=seq,
                          heads=heads_a, dils=tuple(dl for _, dl in groups))

    caches = (cache_kv_w128, cache_kv_w512, cache_kv_w2048)
    q_s, k_s, v_s = (t[n_p:n_tok].reshape(bs, 1, ng, heads_a, hd_a) for t in (q, k_sh, v_sh))
    outs, lses = [], []
    for gi, (win, dil) in enumerate(groups):
        k_all = jnp.concatenate([caches[gi][:, :, 0].astype(F32), k_s[:, :, gi]], axis=1)
        v_all = jnp.concatenate([caches[gi][:, :, 1].astype(F32), v_s[:, :, gi]], axis=1)
        o, lse = _sample_attn_group(q_s[:, :, gi], k_all, v_all, dil, win // dil)
        outs.append(o)
        lses.append(lse)
    wts = jax.nn.softmax(jnp.stack(lses), axis=0)
    att_s = jnp.sum(wts[..., None] * jnp.stack(outs), axis=0).reshape(bs, heads_a * hd_a).astype(BF16)
    att = pad_rows(jnp.concatenate([att_p, att_s], 0))
    mix1 = mm(att, attn_w_out, 0, tn=1024, tk=512, name="attn_out")
    x3, x3b = _res_ln(x2, mix1, ln_g[1, 0], ln_b[1, 0], alpha=alpha, tm=208, name="ln_1a")

    n_exp = moe_router.shape[-1]
    f1 = _moe(x3, x3b, n_tok, moe_router.reshape(d, n_exp), moe_w_in.reshape(n_exp, d, -1),
              moe_w_out.reshape(n_exp, -1, d))
    x4, _ = _res_ln(x3, f1, ln_g[1, 1], ln_b[1, 1], alpha=alpha, tm=208, name="ln_1b")

    y_prompt = x4[:n_p].reshape(bp, seq, d)
    y_sample = x4[n_p:n_tok].reshape(bs, 1, d)
    prompt_shift = x_prompt[:, -1][None]
    sample_shift = x_sample[:, -1][None]
    k_p = k_sh[:n_p].reshape(bp, seq, ng, heads_a, hd_a)
    v_p = v_sh[:n_p].reshape(bp, seq, ng, heads_a, hd_a)
    prompt_kv, sample_kv = [], []
    for gi, (win, dil) in enumerate(groups):
        keep = min(win, seq)
        prompt_kv.append(jnp.stack([k_p[:, seq - keep:, gi], v_p[:, seq - keep:, gi]], axis=2))
        sample_kv.append(jnp.stack([k_s[:, :, gi], v_s[:, :, gi]], axis=2))
    return (y_prompt, y_sample, prompt_wkv, prompt_shift, prompt_kv[0], prompt_kv[1], prompt_kv[2],
            sample_wkv, sample_shift, sample_kv[0], sample_kv[1], sample_kv[2])
```

```python
import functools

import jax
import jax.numpy as jnp
from jax import lax
from jax.experimental import pallas as pl
from jax.experimental.pallas import tpu as pltpu

F32 = jnp.float32
BF16 = jnp.bfloat16

LANES = 128
SUBLANES = 8
VMEM_LIMIT = 56 * 1024 * 1024

LN_EPS = 1e-5
ROPE_THETA = 10000.0
PAST_LEN = 16384
MOE_ROWS = 512


def _cparams(sem):
    return pltpu.CompilerParams(dimension_semantics=sem, vmem_limit_bytes=VMEM_LIMIT)


def _mm_body(x_ref, w_ref, *rest, nk, epilogue, has_bias):
    if has_bias:
        b_ref, o_ref, acc_ref = rest
    else:
        o_ref, acc_ref = rest
    k = pl.program_id(2)

    @pl.when(k == 0)
    def _init():
        acc_ref[...] = jnp.zeros_like(acc_ref)

    acc_ref[...] += jnp.dot(x_ref[...].astype(BF16), w_ref[...].astype(BF16),
                            preferred_element_type=F32)

    @pl.when(k == nk - 1)
    def _fin():
        y = acc_ref[...]
        if has_bias:
            y = y + b_ref[...]
        if epilogue is not None:
            y = epilogue(y)
        o_ref[...] = y.astype(o_ref.dtype)


def _matmul(x, w3, li, *, tm, tn, tk, out_dtype=F32, bias=None, epilogue=None, name="mm"):
    m, kdim = x.shape
    _, kw, n = w3.shape
    assert kw == kdim and m % tm == 0 and n % tn == 0 and kdim % tk == 0
    nk = kdim // tk
    in_specs = [pl.BlockSpec((tm, tk), lambda i, j, k: (i, k)),
                pl.BlockSpec((None, tk, tn), lambda i, j, k: (li, k, j))]
    args = [x, w3]
    if bias is not None:
        in_specs.append(pl.BlockSpec((1, tn), lambda i, j, k: (0, j)))
        args.append(bias.reshape(1, n).astype(F32))
    return pl.pallas_call(
        functools.partial(_mm_body, nk=nk, epilogue=epilogue, has_bias=bias is not None),
        grid=(m // tm, n // tn, nk),
        in_specs=in_specs,
        out_specs=pl.BlockSpec((tm, tn), lambda i, j, k: (i, j)),
        out_shape=jax.ShapeDtypeStruct((m, n), out_dtype),
        scratch_shapes=[pltpu.VMEM((tm, tn), F32)],
        compiler_params=_cparams(("parallel", "parallel", "arbitrary")),
        name=name,
    )(*args)


def _mm_heads_body(x_ref, w_ref, cos_ref, sin_ref, o_ref, acc_ref, *, nk, n_rope, hd):
    j = pl.program_id(1)
    k = pl.program_id(2)

    @pl.when(k == 0)
    def _init():
        acc_ref[...] = jnp.zeros_like(acc_ref)

    acc_ref[...] += jnp.dot(x_ref[...].astype(BF16), w_ref[...].astype(BF16),
                            preferred_element_type=F32)
    n_heads = acc_ref.shape[1] // hd

    @pl.when((k == nk - 1) & (j < n_rope))
    def _fin_rope():
        cos, sin = cos_ref[...], sin_ref[...]
        for h in range(n_heads):
            y = acc_ref[:, h * hd:(h + 1) * hd]
            o_ref[h] = y * cos + pltpu.roll(y, hd // 2, 1) * sin

    @pl.when((k == nk - 1) & (j >= n_rope))
    def _fin_plain():
        for h in range(n_heads):
            o_ref[h] = acc_ref[:, h * hd:(h + 1) * hd]


def _matmul_heads(x, w3, li, cos, sin_s, *, n_rope, tm, tn, tk, hd, name):
    m, kdim = x.shape
    _, _, n = w3.shape
    assert m % tm == 0 and n % tn == 0 and kdim % tk == 0 and tn % hd == 0
    nk = kdim // tk
    hpt = tn // hd
    return pl.pallas_call(
        functools.partial(_mm_heads_body, nk=nk, n_rope=n_rope, hd=hd),
        grid=(m // tm, n // tn, nk),
        in_specs=[pl.BlockSpec((tm, tk), lambda i, j, k: (i, k)),
                  pl.BlockSpec((None, tk, tn), lambda i, j, k: (li, k, j)),
                  pl.BlockSpec((tm, hd), lambda i, j, k: (i, 0)),
                  pl.BlockSpec((tm, hd), lambda i, j, k: (i, 0))],
        out_specs=pl.BlockSpec((hpt, tm, hd), lambda i, j, k: (j, i, 0)),
        out_shape=jax.ShapeDtypeStruct((n // hd, m, hd), F32),
        scratch_shapes=[pltpu.VMEM((tm, tn), F32)],
        compiler_params=_cparams(("parallel", "parallel", "arbitrary")),
        name=name,
    )(x, w3, cos, sin_s)


def _swiglu_body(x_ref, wg_ref, wu_ref, o_ref, accg_ref, accu_ref, *, nk):
    k = pl.program_id(2)

    @pl.when(k == 0)
    def _init():
        accg_ref[...] = jnp.zeros_like(accg_ref)
        accu_ref[...] = jnp.zeros_like(accu_ref)

    xb = x_ref[...].astype(BF16)
    accg_ref[...] += jnp.dot(xb, wg_ref[...].astype(BF16), preferred_element_type=F32)
    accu_ref[...] += jnp.dot(xb, wu_ref[...].astype(BF16), preferred_element_type=F32)

    @pl.when(k == nk - 1)
    def _fin():
        g = accg_ref[...]
        o_ref[...] = (g * jax.nn.sigmoid(g) * accu_ref[...]).astype(o_ref.dtype)


def _swiglu_in(x, w3, li, *, tm, tn, tk, name="swiglu_in"):
    m, kdim = x.shape
    _, _, n2 = w3.shape
    hid = n2 // 2
    assert m % tm == 0 and hid % tn == 0 and kdim % tk == 0
    nk = kdim // tk
    nt = hid // tn
    return pl.pallas_call(
        functools.partial(_swiglu_body, nk=nk),
        grid=(m // tm, nt, nk),
        in_specs=[pl.BlockSpec((tm, tk), lambda i, j, k: (i, k)),
                  pl.BlockSpec((None, tk, tn), lambda i, j, k: (li, k, j)),
                  pl.BlockSpec((None, tk, tn), lambda i, j, k: (li, k, j + nt))],
        out_specs=pl.BlockSpec((tm, tn), lambda i, j, k: (i, j)),
        out_shape=jax.ShapeDtypeStruct((m, hid), BF16),
        scratch_shapes=[pltpu.VMEM((tm, tn), F32), pltpu.VMEM((tm, tn), F32)],
        compiler_params=_cparams(("parallel", "parallel", "arbitrary")),
        name=name,
    )(x, w3, w3)


def _ln_body(x_ref, f_ref, g_ref, b_ref, o_ref, ob_ref, *, alpha):
    z = alpha * x_ref[...] + f_ref[...]
    mu = jnp.mean(z, -1, keepdims=True)
    zc = z - mu
    var = jnp.mean(zc * zc, -1, keepdims=True)
    y = zc * lax.rsqrt(var + LN_EPS) * g_ref[...] + b_ref[...]
    o_ref[...] = y
    ob_ref[...] = y.astype(BF16)


def _res_ln(x, f, g, b, *, alpha, tm, name="res_ln"):
    m, d = x.shape
    assert m % tm == 0
    row = pl.BlockSpec((tm, d), lambda i: (i, 0))
    vec = pl.BlockSpec((1, d), lambda i: (0, 0))
    return pl.pallas_call(
        functools.partial(_ln_body, alpha=alpha),
        grid=(m // tm,),
        in_specs=[row, row, vec, vec],
        out_specs=[row, row],
        out_shape=[jax.ShapeDtypeStruct((m, d), F32), jax.ShapeDtypeStruct((m, d), BF16)],
        compiler_params=_cparams(("parallel",)),
        name=name,
    )(x, f, g.reshape(1, d), b.reshape(1, d))


def _wkv_body(r_ref, w_ref, k_ref, v_ref, a_ref, b_ref, s0_ref, y_ref, s_ref, *, tc, n):
    t_blk = pl.program_id(1)

    @pl.when(t_blk == 0)
    def _init():
        s_ref[...] = s0_ref[...]

    def row(ref, t, j):
        return ref[t, pl.ds(j, 1), :]

    def step(t, carry):
        sa = jnp.zeros((n, LANES), F32)
        for j in range(n):
            sa = sa + s_ref[j] * row(a_ref, t, j)
        v_t = v_ref[t]
        y = jnp.zeros((n, LANES), F32)
        for j in range(n):
            s_new = s_ref[j] * row(w_ref, t, j) + sa * row(b_ref, t, j) + v_t * row(k_ref, t, j)
            s_ref[j] = s_new
            y = y + s_new * row(r_ref, t, j)
        y_ref[t] = y
        return carry

    lax.fori_loop(0, tc, step, 0)


def _wkv_scan(r, w, k, v, a, b, s0, *, tc, name="wkv_scan"):
    t_len, n, lanes = r.shape
    assert lanes % LANES == 0 and t_len % tc == 0
    seq = pl.BlockSpec((tc, n, LANES), lambda l, t: (t, 0, l))
    st = pl.BlockSpec((n, n, LANES), lambda l, t: (0, 0, l))
    return pl.pallas_call(
        functools.partial(_wkv_body, tc=tc, n=n),
        grid=(lanes // LANES, t_len // tc),
        in_specs=[seq] * 6 + [st],
        out_specs=[seq, st],
        out_shape=[jax.ShapeDtypeStruct((t_len, n, lanes), F32),
                   jax.ShapeDtypeStruct((n, n, lanes), F32)],
        compiler_params=_cparams(("parallel", "arbitrary")),
        name=name,
    )(r, w, k, v, a, b, s0)


CHUNK = 128
SUB = 32


def _head_normalize(x_t, hl, n):
    x3 = x_t.reshape(hl, n, CHUNK)
    nrm = jnp.sqrt(jnp.sum(x3 * x3, axis=1, keepdims=True))
    return (x3 / jnp.maximum(nrm, 1e-12)).reshape(hl * n, CHUNK)


def _to_scan_body(*refs, mode, nb, hl, n):
    n_in = {"copy": 1, "k2": 2, "kk": 1, "kb": 2}[mode]
    tiles = [refs[i * nb:(i + 1) * nb] for i in range(n_in)]
    pos = n_in * nb
    par_ref = None
    if mode != "copy":
        par_ref = refs[pos]
        pos += 1
    o_ref, sc_ref = refs[pos], refs[pos + 1]

    for b in range(nb):
        if mode == "copy":
            res = tiles[0][b][...].T
        elif mode == "k2":
            k, a = tiles[0][b][...], tiles[1][b][...]
            res = (k * (1.0 + (a - 1.0) * par_ref[...])).T
        elif mode == "kk":
            res = _head_normalize((tiles[0][b][...] * par_ref[...]).T, hl, n)
        else:
            kk = _head_normalize((tiles[0][b][...] * par_ref[...]).T, hl, n)
            res = kk * tiles[1][b][...].T
        sc_ref[b] = res

    for j in range(n):
        slab = jnp.concatenate([sc_ref.at[b][pl.ds(j, hl, stride=n), :] for b in range(nb)], axis=0)
        slab_t = slab.T
        for s in range(CHUNK // SUB):
            o_ref[s, pl.ds(j * SUB, SUB), :] = slab_t[s * SUB:(s + 1) * SUB, :]


def _to_scan(inputs, par, *, mode, nb, seq, heads, n, name):
    hl = LANES // nb
    width = hl * n
    n_hh = heads // hl
    n_c = seq // CHUNK
    in_specs, args = [], []
    for x in inputs:
        for b in range(nb):
            in_specs.append(pl.BlockSpec((CHUNK, width), lambda hh, c, b=b: (b * n_c + c, hh)))
            args.append(x)
    if par is not None:
        in_specs.append(pl.BlockSpec((1, width), lambda hh, c: (0, hh)))
        args.append(par.reshape(1, heads * n))
    return pl.pallas_call(
        functools.partial(_to_scan_body, mode=mode, nb=nb, hl=hl, n=n),
        grid=(n_hh, n_c),
        in_specs=in_specs,
        out_specs=pl.BlockSpec((CHUNK // SUB, n * SUB, LANES), lambda hh, c: (c, 0, hh)),
        out_shape=jax.ShapeDtypeStruct((seq // SUB, n * SUB, n_hh * LANES), F32),
        scratch_shapes=[pltpu.VMEM((nb, width, CHUNK), F32)],
        compiler_params=_cparams(("parallel", "parallel")),
        name=name,
    )(*args)


def _wkv_sl_body(r_ref, w_ref, k_ref, v_ref, kk_ref, b_ref, y_ref, s_ref, *, n):
    @pl.when(pl.program_id(1) == 0)
    def _init():
        s_ref[...] = jnp.zeros_like(s_ref)

    def row(ref, t, j):
        return ref[pl.ds(j * SUB + t, 1), :]

    def step(t, skk):
        t_next = jnp.minimum(t + 1, SUB - 1)
        v_t = v_ref[pl.ds(t, n, stride=SUB), :]
        y = jnp.zeros((n, LANES), F32)
        skk_next = jnp.zeros((n, LANES), F32)
        for j in range(n):
            s_new = s_ref[j] * row(w_ref, t, j) - skk * row(b_ref, t, j) + v_t * row(k_ref, t, j)
            s_ref[j] = s_new
            y = y + s_new * row(r_ref, t, j)
            skk_next = skk_next + s_new * row(kk_ref, t_next, j)
        y_ref[pl.ds(t, n, stride=SUB), :] = y
        return skk_next

    skk0 = jnp.zeros((n, LANES), F32)
    for j in range(n):
        skk0 = skk0 + s_ref[j] * row(kk_ref, 0, j)
    lax.fori_loop(0, SUB, step, skk0)


def _wkv_scan_sl(r, w, k, v, kk, b, *, n, name="wkv_scan_prompt"):
    n_blk, rows, lanes = r.shape
    seq = pl.BlockSpec((None, rows, LANES), lambda l, t: (t, 0, l))
    st = pl.BlockSpec((n, n, LANES), lambda l, t: (0, 0, l))
    return pl.pallas_call(
        functools.partial(_wkv_sl_body, n=n),
        grid=(lanes // LANES, n_blk),
        in_specs=[seq] * 6,
        out_specs=[seq, st],
        out_shape=[jax.ShapeDtypeStruct((n_blk, rows, lanes), F32),
                   jax.ShapeDtypeStruct((n, n, lanes), F32)],
        compiler_params=_cparams(("parallel", "arbitrary")),
        name=name,
    )(r, w, k, v, kk, b)


def _from_scan_body(y_ref, r_ref, k_ref, v_ref, rk_ref, lg_ref, lb_ref, g_ref, o_ref, z_ref, sc_ref,
                    *, nb, hl, n, eps):
    b = pl.program_id(2)

    @pl.when(b == 0)
    def _fill():
        for s in range(CHUNK // SUB):
            def blk(ref, j, s=s):
                return ref[s, pl.ds(j * SUB, SUB), :]

            coef = jnp.zeros((SUB, LANES), F32)
            tot = jnp.zeros((SUB, LANES), F32)
            for j in range(n):
                coef = coef + blk(r_ref, j) * blk(k_ref, j) * rk_ref[pl.ds(j, 1), :]
                tot = tot + blk(y_ref, j)
            mu = tot / n
            var = jnp.zeros((SUB, LANES), F32)
            for j in range(n):
                d = blk(y_ref, j) - mu
                var = var + d * d
            inv = lax.rsqrt(var / n + eps)
            for j in range(n):
                z = (blk(y_ref, j) - mu) * inv * lg_ref[pl.ds(j, 1), :] + lb_ref[pl.ds(j, 1), :]
                z_ref[j, pl.ds(s * SUB, SUB), :] = z + coef * blk(v_ref, j)
        for j in range(n):
            slab_t = z_ref[j].T
            for bb in range(nb):
                sc_ref.at[bb][pl.ds(j, hl, stride=n), :] = slab_t[bb * hl:(bb + 1) * hl, :]

    o_ref[...] = (sc_ref[b].T * g_ref[...]).astype(o_ref.dtype)


def _from_scan(y, r, k2, v, rk_t, lg_t, lb_t, g, *, nb, seq, heads, n, eps, name="rwkv_post"):
    hl = LANES // nb
    width = hl * n
    n_hh = heads // hl
    n_c = seq // CHUNK
    sl = pl.BlockSpec((CHUNK // SUB, n * SUB, LANES), lambda hh, c, b: (c, 0, hh))
    par = pl.BlockSpec((n, LANES), lambda hh, c, b: (0, hh))
    tok = pl.BlockSpec((CHUNK, width), lambda hh, c, b: (b * n_c + c, hh))
    return pl.pallas_call(
        functools.partial(_from_scan_body, nb=nb, hl=hl, n=n, eps=eps),
        grid=(n_hh, n_c, nb),
        in_specs=[sl, sl, sl, sl, par, par, par, tok],
        out_specs=tok,
        out_shape=jax.ShapeDtypeStruct((nb * seq, heads * n), BF16),
        scratch_shapes=[pltpu.VMEM((n, CHUNK, LANES), F32), pltpu.VMEM((nb, width, CHUNK), F32)],
        compiler_params=_cparams(("parallel", "parallel", "arbitrary")),
        name=name,
    )(y, r, k2, v, rk_t, lg_t, lb_t, g)


def _head_param_lanes(p, nb, heads, n):
    hl = LANES // nb
    t = p.reshape(heads // hl, 1, hl, n)
    t = jnp.broadcast_to(t, (heads // hl, nb, hl, n))
    return jnp.transpose(t, (3, 0, 1, 2)).reshape(n, (heads // hl) * LANES).astype(F32)


SPAN = 128


def _attn_block(q, k, v, causal_own, scale):
    s = lax.dot_general(q.astype(BF16), k.astype(BF16), (((1,), (1,)), ((), ())),
                        preferred_element_type=F32) * scale
    qi = lax.broadcasted_iota(jnp.int32, s.shape, 0)
    ki = lax.broadcasted_iota(jnp.int32, s.shape, 1)
    if causal_own:
        mask = ki <= qi
    else:
        mask = (ki >= qi) & (ki <= qi + SPAN)
    s = jnp.where(mask, s, -jnp.inf)
    m = jnp.max(s, axis=1, keepdims=True)
    p = jnp.exp(s - m)
    l = jnp.sum(p, axis=1, keepdims=True)
    o = jnp.dot(p.astype(BF16), v.astype(BF16), preferred_element_type=F32)
    return o, m, l


def _dilated_attn_body(*refs, dils, seq, scale):
    ng = len(dils)
    q_refs, k_refs, v_refs = refs[:ng], refs[ng:2 * ng], refs[2 * ng:3 * ng]
    o_ref = refs[3 * ng]
    scratch = refs[3 * ng + 1:]
    acc_refs, m_refs, l_refs = scratch[:ng], scratch[ng:2 * ng], scratch[2 * ng:]

    for g, dil in enumerate(dils):
        q_ref, k_ref, v_ref = q_refs[g], k_refs[g], v_refs[g]
        acc_ref, m_ref, l_ref = acc_refs[g], m_refs[g], l_refs[g]
        nblk = seq // (dil * SPAN)

        def rows(start, size, dil=dil):
            if dil == 1:
                return pl.ds(start, size)
            return pl.ds(start, size, stride=dil)

        def put(sl, res, acc_ref=acc_ref, m_ref=m_ref, l_ref=l_ref):
            o, m, l = res
            acc_ref[sl, :] = o
            m_ref[sl, :] = m
            l_ref[sl, :] = l

        for r in range(dil):
            sl0 = rows(r, SPAN)
            put(sl0, _attn_block(q_ref[sl0, :], k_ref[sl0, :], v_ref[sl0, :], True, scale))

            def body(nb, carry, r=r, dil=dil, rows=rows, put=put, q_ref=q_ref, k_ref=k_ref, v_ref=v_ref):
                q_sl = rows(r + dil * SPAN * nb, SPAN)
                kv_sl = rows(r + dil * SPAN * (nb - 1), 2 * SPAN)
                put(q_sl, _attn_block(q_ref[q_sl, :], k_ref[kv_sl, :], v_ref[kv_sl, :], False, scale))
                return carry

            if nblk > 1:
                lax.fori_loop(1, nblk, body, 0)

    m_all = m_refs[0][...]
    for g in range(1, ng):
        m_all = jnp.maximum(m_all, m_refs[g][...])
    num = jnp.zeros(acc_refs[0].shape, F32)
    den = jnp.zeros(m_all.shape, F32)
    for g in range(ng):
        e = jnp.exp(m_refs[g][...] - m_all)
        num = num + e * acc_refs[g][...]
        den = den + e * l_refs[g][...]
    o_ref[...] = (num / den).astype(o_ref.dtype)


def _dilated_attn(q_hm, k_hm, v_hm, *, batch, seq, heads, dils, v_head0=0, name="dilated_attn"):
    e = q_hm.shape[-1]
    ng = len(dils)

    def spec(g, h0=0):
        return pl.BlockSpec((None, seq, e), lambda b, h, g=g: (h0 + g * heads + h, b, 0))

    specs = [spec(g) for g in range(ng)]
    return pl.pallas_call(
        functools.partial(_dilated_attn_body, dils=dils, seq=seq, scale=e ** -0.5),
        grid=(batch, heads),
        in_specs=specs * 2 + [spec(g, v_head0) for g in range(ng)],
        out_specs=pl.BlockSpec((seq, e), lambda b, h: (b, h)),
        out_shape=jax.ShapeDtypeStruct((batch * seq, heads * e), BF16),
        scratch_shapes=([pltpu.VMEM((seq, e), F32)] * ng + [pltpu.VMEM((seq, 1), F32)] * (2 * ng)),
        compiler_params=_cparams(("parallel", "parallel")),
        name=name,
    )(*([q_hm] * ng + [k_hm] * ng + [v_hm] * ng))


def _router_body(x_ref, w_ref, o_ref):
    o_ref[...] = jnp.dot(x_ref[...], w_ref[...], precision=lax.Precision.HIGHEST,
                         preferred_element_type=F32)


def _router_logits(x, router, *, tm, name="router_logits"):
    m, d = x.shape
    n_exp = router.shape[1]
    w = jnp.zeros((d, LANES), F32).at[:, :n_exp].set(router.astype(F32))
    out = pl.pallas_call(
        _router_body,
        grid=(m // tm,),
        in_specs=[pl.BlockSpec((tm, d), lambda i: (i, 0)), pl.BlockSpec((d, LANES), lambda i: (0, 0))],
        out_specs=pl.BlockSpec((tm, LANES), lambda i: (i, 0)),
        out_shape=jax.ShapeDtypeStruct((m, LANES), F32),
        compiler_params=_cparams(("parallel",)),
        name=name,
    )(x, w)
    return out[:, :n_exp]


def _moe_in_body(be_ref, nu_ref, x_ref, wg_ref, wu_ref, o_ref, wgb_ref, wub_ref):
    r = pl.program_id(1)
    changed = (r == 0) | (be_ref[r] != be_ref[jnp.maximum(r - 1, 0)])

    @pl.when(changed)
    def _cast():
        wgb_ref[...] = wg_ref[...].astype(BF16)
        wub_ref[...] = wu_ref[...].astype(BF16)

    @pl.when(r < nu_ref[0])
    def _compute():
        x = x_ref[...]
        g = jnp.dot(x, wgb_ref[...], preferred_element_type=F32)
        u = jnp.dot(x, wub_ref[...], preferred_element_type=F32)
        o_ref[...] = (g * jax.nn.sigmoid(g) * u).astype(o_ref.dtype)

    @pl.when(r >= nu_ref[0])
    def _skip():
        o_ref[...] = jnp.zeros_like(o_ref)


def _moe_in(x_buf, w_in, blk_e, n_used, *, tn, name="moe_swiglu_in"):
    rows, d = x_buf.shape
    n_exp, _, n2 = w_in.shape
    hid = n2 // 2
    nt = hid // tn
    rb = rows // MOE_ROWS

    def x_map(j, r, be, nu):
        return (jnp.minimum(r, nu[0] - 1), 0)

    grid_spec = pltpu.PrefetchScalarGridSpec(
        num_scalar_prefetch=2,
        grid=(nt, rb),
        in_specs=[pl.BlockSpec((MOE_ROWS, d), x_map),
                  pl.BlockSpec((None, d, tn), lambda j, r, be, nu: (be[r], 0, j)),
                  pl.BlockSpec((None, d, tn), lambda j, r, be, nu: (be[r], 0, j + nt))],
        out_specs=pl.BlockSpec((MOE_ROWS, tn), lambda j, r, be, nu: (r, j)),
        scratch_shapes=[pltpu.VMEM((d, tn), BF16), pltpu.VMEM((d, tn), BF16)],
    )
    return pl.pallas_call(
        _moe_in_body,
        grid_spec=grid_spec,
        out_shape=jax.ShapeDtypeStruct((rows, hid), BF16),
        compiler_params=_cparams(("arbitrary", "arbitrary")),
        name=name,
    )(blk_e, n_used, x_buf, w_in, w_in)


def _moe_out_body(be_ref, nu_ref, h_ref, w_ref, o_ref, acc_ref, *, nk):
    r = pl.program_id(0)
    k = pl.program_id(1)

    @pl.when(k == 0)
    def _init():
        acc_ref[...] = jnp.zeros_like(acc_ref)

    @pl.when(r < nu_ref[0])
    def _compute():
        acc_ref[...] += jnp.dot(h_ref[...], w_ref[...].astype(BF16), preferred_element_type=F32)

    @pl.when(k == nk - 1)
    def _fin():
        o_ref[...] = acc_ref[...]


def _moe_out(h_buf, w_out, blk_e, n_used, *, tk, name="moe_out"):
    rows, hid = h_buf.shape
    n_exp, _, d = w_out.shape
    rb = rows // MOE_ROWS
    nk = hid // tk

    def k_eff(r, k, nu):
        return jnp.where(r < nu[0], k, nk - 1)

    grid_spec = pltpu.PrefetchScalarGridSpec(
        num_scalar_prefetch=2,
        grid=(rb, nk),
        in_specs=[pl.BlockSpec((MOE_ROWS, tk),
                               lambda r, k, be, nu: (jnp.minimum(r, nu[0] - 1), k_eff(r, k, nu))),
                  pl.BlockSpec((None, tk, d), lambda r, k, be, nu: (be[r], k_eff(r, k, nu), 0))],
        out_specs=pl.BlockSpec((MOE_ROWS, d), lambda r, k, be, nu: (r, 0)),
        scratch_shapes=[pltpu.VMEM((MOE_ROWS, d), F32)],
    )
    return pl.pallas_call(
        functools.partial(_moe_out_body, nk=nk),
        grid_spec=grid_spec,
        out_shape=jax.ShapeDtypeStruct((rows, d), F32),
        compiler_params=_cparams(("arbitrary", "arbitrary")),
        name=name,
    )(blk_e, n_used, h_buf, w_out)


def _moe(x_f32, n_tok, router, w_in, w_out):
    rows, d = x_f32.shape
    n_exp = router.shape[1]
    top_k = 2
    logits = _router_logits(x_f32, router, tm=520)[:n_tok]
    top_v, top_i = lax.top_k(logits, top_k)
    gates = jax.nn.softmax(top_v, axis=-1)
    n_asg = n_tok * top_k
    e = top_i.reshape(n_asg).astype(jnp.int32)
    tok = jnp.repeat(jnp.arange(n_tok, dtype=jnp.int32), top_k)
    order = jnp.argsort(e)
    e_s, tok_s = e[order], tok[order]
    counts = jnp.bincount(e, length=n_exp).astype(jnp.int32)
    starts = jnp.cumsum(counts) - counts
    padded = -(-counts // MOE_ROWS) * MOE_ROWS
    padded_ends = jnp.cumsum(padded)
    padded_starts = padded_ends - padded
    dest = (padded_starts[e_s] + jnp.arange(n_asg, dtype=jnp.int32) - starts[e_s]).astype(jnp.int32)
    n_blk = -(-n_asg // MOE_ROWS) + n_exp
    src_tok = jnp.zeros((n_blk * MOE_ROWS,), jnp.int32).at[dest].set(tok_s)
    pos = jnp.zeros((n_asg,), jnp.int32).at[order].set(dest).reshape(n_tok, top_k)
    blk_e = jnp.minimum(jnp.searchsorted(padded_ends, jnp.arange(n_blk, dtype=jnp.int32) * MOE_ROWS,
                                         side='right'), n_exp - 1).astype(jnp.int32)
    n_used = (padded_ends[-1:] // MOE_ROWS).astype(jnp.int32)

    x_buf = x_f32[src_tok].astype(BF16)
    h_buf = _moe_in(x_buf, w_in, blk_e, n_used, tn=512)
    y_buf = _moe_out(h_buf, w_out, blk_e, n_used, tk=512)
    y = y_buf[pos[:, 0]] * gates[:, 0:1] + y_buf[pos[:, 1]] * gates[:, 1:2]
    return jnp.zeros((rows, d), F32).at[:n_tok].set(y)


def _rope_tables(pos, hd):
    inv = ROPE_THETA ** (-jnp.arange(0, hd, 2, dtype=F32) / hd)
    ang = pos.astype(F32)[:, None] * inv[None, :]
    ang = jnp.concatenate([ang, ang], -1)
    return jnp.cos(ang), jnp.sin(ang)


def _softplus(z):
    return jnp.maximum(z, 0.0) + jnp.log(1.0 + jnp.exp(-jnp.abs(z)))


def _decay_epilogue(y):
    return jnp.exp(-jnp.exp(-_softplus(-y) - 0.5))


def _sample_attn_group(q, k_all, v_all, dil, span):
    n_new = q.shape[1]
    length = k_all.shape[1]
    q_idx = length - n_new + jnp.arange(n_new)
    idx = q_idx[:, None] - dil * jnp.arange(span + 1)[None, :]
    valid = idx >= 0
    idx = jnp.maximum(idx, 0)
    kg = k_all[:, idx]
    vg = v_all[:, idx]
    s = jnp.einsum('bjhe,bjkhe->bhjk', q, kg, preferred_element_type=F32) * (q.shape[-1] ** -0.5)
    s = jnp.where(valid[None, None], s, -jnp.inf)
    m = jnp.max(s, -1, keepdims=True)
    p = jnp.exp(s - m)
    l = jnp.sum(p, -1, keepdims=True)
    o = jnp.einsum('bhjk,bjkhe->bhje', p, vg) / l
    lse = (m + jnp.log(l))[..., 0]
    return o.transpose(0, 2, 1, 3), lse.transpose(0, 2, 1)


def kernel(x_prompt, x_sample, state_wkv, state_shift, cache_kv_w128, cache_kv_w512, cache_kv_w2048,
           ln_g, ln_b, rwkv_mix, rwkv_w_rkv, rwkv_w0, rwkv_w1, rwkv_w2, rwkv_a0, rwkv_a1, rwkv_a2,
           rwkv_g1, rwkv_g2, rwkv_k_k, rwkv_k_a, rwkv_r_k, rwkv_lnx_g, rwkv_lnx_b, rwkv_w_out,
           attn_w_kv, attn_w_q, attn_w_out, ffn_w_in, ffn_w_out, moe_router, moe_w_in, moe_w_out):
    bp, seq, d = x_prompt.shape
    bs, seq_s, _ = x_sample.shape
    depth = ln_g.shape[0]
    assert depth == 2 and seq_s == 1
    heads_r, hd_r = rwkv_r_k.shape[1], rwkv_r_k.shape[2]
    past_len = PAST_LEN
    groups = ((128, 1), (512, 4), (2048, 16))
    ng = len(groups)
    hd_a = 128
    heads_a = attn_w_out.shape[1] // hd_a
    alpha = (2.0 * depth) ** 0.25
    lnx_eps = 1e-5 * hd_r

    n_p = bp * seq
    n_tok = n_p + bs
    tm = 2080
    rows = -(-n_tok // tm) * tm

    def pad_rows(a):
        return jnp.concatenate([a, jnp.zeros((rows - a.shape[0],) + a.shape[1:], a.dtype)], 0)

    x0 = pad_rows(jnp.concatenate([x_prompt.reshape(n_p, d), x_sample.reshape(bs, d)], 0))
    prev_p = jnp.concatenate([jnp.zeros((bp, 1, d), F32), x_prompt[:, :-1]], 1).reshape(n_p, d)
    x_prev = pad_rows(jnp.concatenate([prev_p, state_shift[0]], 0))
    xx = x_prev - x0
    xs = [(x0 + xx * rwkv_mix[0, c]).astype(BF16) for c in range(6)]

    w_rkv = rwkv_w_rkv.reshape(3, d, d)
    mm = functools.partial(_matmul, tm=tm)
    r = mm(xs[0], w_rkv, 0, tn=1024, tk=1024, name="rwkv_r")
    k = mm(xs[1], w_rkv, 1, tn=1024, tk=1024, name="rwkv_k")
    v = mm(xs[2], w_rkv, 2, tn=1024, tk=1024, name="rwkv_v")
    lw, la, lg = rwkv_w1.shape[2], rwkv_a1.shape[2], rwkv_g1.shape[2]
    hw = mm(xs[3], rwkv_w1, 0, tn=lw, tk=1024, out_dtype=BF16, epilogue=jnp.tanh, name="rwkv_w1")
    decay = mm(hw, rwkv_w2, 0, tn=1024, tk=lw, bias=rwkv_w0[0], epilogue=_decay_epilogue, name="rwkv_w2")
    ha = mm(xs[4], rwkv_a1, 0, tn=la, tk=1024, out_dtype=BF16, name="rwkv_a1")
    a = mm(ha, rwkv_a2, 0, tn=1024, tk=la, bias=rwkv_a0[0], epilogue=jax.nn.sigmoid, name="rwkv_a2")
    hg = mm(xs[5], rwkv_g1, 0, tn=lg, tk=1024, out_dtype=BF16, epilogue=jax.nn.sigmoid, name="rwkv_g1")
    g = mm(hg, rwkv_g2, 0, tn=1024, tk=lg, name="rwkv_g2")

    hl = LANES // bp
    assert LANES % bp == 0 and heads_r % hl == 0 and seq % CHUNK == 0
    kw = dict(nb=bp, seq=seq, heads=heads_r, n=hd_r)
    r_sl = _to_scan([r], None, mode="copy", name="sl_r", **kw)
    w_sl = _to_scan([decay], None, mode="copy", name="sl_w", **kw)
    v_sl = _to_scan([v], None, mode="copy", name="sl_v", **kw)
    k_sl = _to_scan([k, a], rwkv_k_a[0], mode="k2", name="sl_k", **kw)
    kk_sl = _to_scan([k], rwkv_k_k[0], mode="kk", name="sl_kk", **kw)
    b_sl = _to_scan([k, a], rwkv_k_k[0], mode="kb", name="sl_b", **kw)
    y_sl, s_p = _wkv_scan_sl(r_sl, w_sl, k_sl, v_sl, kk_sl, b_sl, n=hd_r)
    tab = functools.partial(_head_param_lanes, nb=bp, heads=heads_r, n=hd_r)
    y_p = _from_scan(y_sl, r_sl, k_sl, v_sl, tab(rwkv_r_k.reshape(-1)), tab(rwkv_lnx_g[0]),
                     tab(rwkv_lnx_b[0]), g, eps=lnx_eps, **kw)
    s_p = s_p.reshape(hd_r, hd_r, heads_r // hl, bp, hl)
    prompt_wkv = jnp.transpose(s_p, (3, 2, 4, 1, 0)).reshape(1, bp, heads_r, hd_r, hd_r).astype(state_wkv.dtype)

    def hs(t):
        return t[n_p:n_tok].reshape(bs, heads_r, hd_r)

    kk_s = hs(k) * rwkv_k_k[0].reshape(heads_r, hd_r)
    kk_s = kk_s / jnp.maximum(jnp.linalg.norm(kk_s, axis=-1, keepdims=True), 1e-12)
    k2_s = hs(k) * (1.0 + (hs(a) - 1.0) * rwkv_k_a[0].reshape(heads_r, hd_r))
    bonus_s = jnp.sum(hs(r) * k2_s * rwkv_r_k[0], -1, keepdims=True) * hs(v)

    def lanes_s(t):
        return jnp.transpose(t, (2, 0, 1)).reshape(1, hd_r, bs * heads_r)

    scan_s = (hs(r), hs(decay), k2_s, hs(v), -kk_s, kk_s * hs(a))
    s0_s = jnp.transpose(state_wkv[0].astype(F32), (3, 2, 0, 1)).reshape(hd_r, hd_r, bs * heads_r)
    y_s, s_s = _wkv_scan(*[lanes_s(t) for t in scan_s], s0_s, tc=1, name="wkv_scan_sample")
    y_s = jnp.transpose(y_s.reshape(hd_r, bs, heads_r), (1, 2, 0))
    sample_wkv = jnp.transpose(s_s.reshape(hd_r, hd_r, bs, heads_r), (2, 3, 1, 0))[None].astype(state_wkv.dtype)
    mu = jnp.mean(y_s, -1, keepdims=True)
    var = jnp.mean(jnp.square(y_s - mu), -1, keepdims=True)
    y_s = ((y_s - mu) * lax.rsqrt(var + lnx_eps)).reshape(bs, d) * rwkv_lnx_g[0] + rwkv_lnx_b[0]
    y_s = ((y_s + bonus_s.reshape(bs, d)) * g[n_p:n_tok]).astype(BF16)

    y = pad_rows(jnp.concatenate([y_p, y_s], 0))
    mix0 = mm(y, rwkv_w_out, 0, tn=1024, tk=1024, name="rwkv_out")
    x1, x1b = _res_ln(x0, mix0, ln_g[0, 0], ln_b[0, 0], alpha=alpha, tm=208, name="ln_0a")

    h = _swiglu_in(x1b, ffn_w_in, 0, tm=tm, tn=256, tk=1024, name="ffn_in")
    f0 = mm(h, ffn_w_out, 0, tn=1024, tk=256, name="ffn_out")
    x2, x2b = _res_ln(x1, f0, ln_g[0, 1], ln_b[0, 1], alpha=alpha, tm=208, name="ln_0b")

    pos = jnp.concatenate([jnp.tile(jnp.arange(seq, dtype=jnp.int32), bp),
                           jnp.full((bs,), past_len, jnp.int32),
                           jnp.zeros((rows - n_tok,), jnp.int32)])
    cos, sin = _rope_tables(pos, hd_a)
    half_sign = jnp.concatenate([-jnp.ones((hd_a // 2,), F32), jnp.ones((hd_a // 2,), F32)])
    sin_s = sin * half_sign
    n_qh = ng * heads_a
    tn_h = 1024
    kv_hm = _matmul_heads(x2b, attn_w_kv.reshape(1, d, -1), 0, cos, sin_s, n_rope=n_qh * hd_a // tn_h,
                          tm=tm, tn=tn_h, tk=1024, hd=hd_a, name="attn_kv")
    q_hm = _matmul_heads(x2b, attn_w_q, 0, cos, sin_s, n_rope=n_qh * hd_a // tn_h,
                         tm=tm, tn=tn_h, tk=1024, hd=hd_a, name="attn_q")
    att_p = _dilated_attn(q_hm, kv_hm, kv_hm, batch=bp, seq=seq, heads=heads_a,
                          dils=tuple(dl for _, dl in groups), v_head0=n_qh)

    caches = (cache_kv_w128, cache_kv_w512, cache_kv_w2048)
    kv5 = kv_hm.reshape(2, ng, heads_a, rows, hd_a)
    q_s = jnp.transpose(q_hm[:, n_p:n_tok], (1, 0, 2)).reshape(bs, 1, ng, heads_a, hd_a)
    kv_s = jnp.transpose(kv5[:, :, :, n_p:n_tok], (3, 0, 1, 2, 4))
    k_s, v_s = kv_s[:, None, 0], kv_s[:, None, 1]
    outs, lses = [], []
    for gi, (win, dil) in enumerate(groups):
        k_all = jnp.concatenate([caches[gi][:, :, 0].astype(F32), k_s[:, :, gi]], axis=1)
        v_all = jnp.concatenate([caches[gi][:, :, 1].astype(F32), v_s[:, :, gi]], axis=1)
        o, lse = _sample_attn_group(q_s[:, :, gi], k_all, v_all, dil, win // dil)
        outs.append(o)
        lses.append(lse)
    wts = jax.nn.softmax(jnp.stack(lses), axis=0)
    att_s = jnp.sum(wts[..., None] * jnp.stack(outs), axis=0).reshape(bs, heads_a * hd_a).astype(BF16)
    att = pad_rows(jnp.concatenate([att_p, att_s], 0))
    mix1 = mm(att, attn_w_out, 0, tn=1024, tk=1024, name="attn_out")
    x3, x3b = _res_ln(x2, mix1, ln_g[1, 0], ln_b[1, 0], alpha=alpha, tm=208, name="ln_1a")

    n_exp = moe_router.shape[-1]
    f1 = _moe(x3, n_tok, moe_router.reshape(d, n_exp), moe_w_in.reshape(n_exp, d, -1),
              moe_w_out.reshape(n_exp, -1, d))
    x4, _ = _res_ln(x3, f1, ln_g[1, 1], ln_b[1, 1], alpha=alpha, tm=208, name="ln_1b")

    y_prompt = x4[:n_p].reshape(bp, seq, d)
    y_sample = x4[n_p:n_tok].reshape(bs, 1, d)
    prompt_shift = x_prompt[:, -1][None]
    sample_shift = x_sample[:, -1][None]
    prompt_kv, sample_kv = [], []
    for gi, (win, dil) in enumerate(groups):
        keep = min(win, seq)
        last = kv5[:, gi, :, :n_p].reshape(2, heads_a, bp, seq, hd_a)[:, :, :, seq - keep:]
        prompt_kv.append(jnp.transpose(last, (2, 3, 0, 1, 4)))
        sample_kv.append(jnp.stack([k_s[:, :, gi], v_s[:, :, gi]], axis=2))
    return (y_prompt, y_sample, prompt_wkv, prompt_shift, prompt_kv[0], prompt_kv[1], prompt_kv[2],
            sample_wkv, sample_shift, sample_kv[0], sample_kv[1], sample_kv[2])
```

```python
import functools

import jax
import jax.numpy as jnp
from jax import lax
from jax.experimental import pallas as pl
from jax.experimental.pallas import tpu as pltpu

F32 = jnp.float32
BF16 = jnp.bfloat16

LANES = 128
SUBLANES = 8
VMEM_LIMIT = 56 * 1024 * 1024

LN_EPS = 1e-5
ROPE_THETA = 10000.0
PAST_LEN = 16384
MOE_ROWS = 512


def _cparams(sem):
    return pltpu.CompilerParams(dimension_semantics=sem, vmem_limit_bytes=VMEM_LIMIT)


def _mm_body(x_ref, w_ref, *rest, nk, k_rem, epilogue, has_bias):
    if has_bias:
        b_ref, o_ref, acc_ref = rest
    else:
        o_ref, acc_ref = rest
    k = pl.program_id(2)

    @pl.when(k == 0)
    def _init():
        acc_ref[...] = jnp.zeros_like(acc_ref)

    def accumulate(x, w):
        acc_ref[...] += jnp.dot(x.astype(BF16), w.astype(BF16), preferred_element_type=F32)

    if k_rem == 0:
        accumulate(x_ref[...], w_ref[...])
    else:
        @pl.when(k < nk - 1)
        def _full():
            accumulate(x_ref[...], w_ref[...])

        @pl.when(k == nk - 1)
        def _edge():
            x, w = x_ref[...], w_ref[...]
            col = lax.broadcasted_iota(jnp.int32, x.shape, 1)
            row = lax.broadcasted_iota(jnp.int32, w.shape, 0)
            accumulate(jnp.where(col < k_rem, x, jnp.zeros_like(x)),
                       jnp.where(row < k_rem, w, jnp.zeros_like(w)))

    @pl.when(k == nk - 1)
    def _fin():
        y = acc_ref[...]
        if has_bias:
            y = y + b_ref[...]
        if epilogue is not None:
            y = epilogue(y)
        o_ref[...] = y.astype(o_ref.dtype)


def _matmul(x, w3, li, *, tm, tn, tk, out_dtype=F32, bias=None, epilogue=None, name="mm"):
    m, kdim = x.shape
    _, kw, n = w3.shape
    assert kw == kdim and m % tm == 0 and n % tn == 0
    nk = pl.cdiv(kdim, tk)
    in_specs = [pl.BlockSpec((tm, tk), lambda i, j, k: (i, k)),
                pl.BlockSpec((None, tk, tn), lambda i, j, k: (li, k, j))]
    args = [x, w3]
    if bias is not None:
        in_specs.append(pl.BlockSpec((1, tn), lambda i, j, k: (0, j)))
        args.append(bias.reshape(1, n).astype(F32))
    return pl.pallas_call(
        functools.partial(_mm_body, nk=nk, k_rem=kdim % tk, epilogue=epilogue, has_bias=bias is not None),
        grid=(m // tm, n // tn, nk),
        in_specs=in_specs,
        out_specs=pl.BlockSpec((tm, tn), lambda i, j, k: (i, j)),
        out_shape=jax.ShapeDtypeStruct((m, n), out_dtype),
        scratch_shapes=[pltpu.VMEM((tm, tn), F32)],
        compiler_params=_cparams(("parallel", "parallel", "arbitrary")),
        name=name,
    )(*args)


def _mm_heads_body(x_ref, w_ref, cos_ref, sin_ref, o_ref, *rest, nk, n_rope, hd, token_major):
    if token_major:
        o2_ref, acc_ref = rest
    else:
        (acc_ref,) = rest
    j = pl.program_id(1)
    k = pl.program_id(2)

    @pl.when(k == 0)
    def _init():
        acc_ref[...] = jnp.zeros_like(acc_ref)

    acc_ref[...] += jnp.dot(x_ref[...].astype(BF16), w_ref[...].astype(BF16),
                            preferred_element_type=F32)
    n_heads = acc_ref.shape[1] // hd

    def emit(h, y):
        o_ref[h] = y
        if token_major:
            o2_ref[:, h * hd:(h + 1) * hd] = y

    @pl.when((k == nk - 1) & (j < n_rope))
    def _fin_rope():
        cos, sin = cos_ref[...], sin_ref[...]
        for h in range(n_heads):
            y = acc_ref[:, h * hd:(h + 1) * hd]
            emit(h, y * cos + pltpu.roll(y, hd // 2, 1) * sin)

    @pl.when((k == nk - 1) & (j >= n_rope))
    def _fin_plain():
        for h in range(n_heads):
            emit(h, acc_ref[:, h * hd:(h + 1) * hd])


def _matmul_heads(x, w3, li, cos, sin_s, *, n_rope, tm, tn, tk, hd, token_major, name):
    m, kdim = x.shape
    _, _, n = w3.shape
    assert m % tm == 0 and n % tn == 0 and kdim % tk == 0 and tn % hd == 0
    nk = kdim // tk
    hpt = tn // hd
    out_specs = [pl.BlockSpec((hpt, tm, hd), lambda i, j, k: (j, i, 0))]
    out_shape = [jax.ShapeDtypeStruct((n // hd, m, hd), F32)]
    if token_major:
        out_specs.append(pl.BlockSpec((tm, tn), lambda i, j, k: (i, j)))
        out_shape.append(jax.ShapeDtypeStruct((m, n), F32))
    return pl.pallas_call(
        functools.partial(_mm_heads_body, nk=nk, n_rope=n_rope, hd=hd, token_major=token_major),
        grid=(m // tm, n // tn, nk),
        in_specs=[pl.BlockSpec((tm, tk), lambda i, j, k: (i, k)),
                  pl.BlockSpec((None, tk, tn), lambda i, j, k: (li, k, j)),
                  pl.BlockSpec((tm, hd), lambda i, j, k: (i, 0)),
                  pl.BlockSpec((tm, hd), lambda i, j, k: (i, 0))],
        out_specs=out_specs,
        out_shape=out_shape,
        scratch_shapes=[pltpu.VMEM((tm, tn), F32)],
        compiler_params=_cparams(("parallel", "parallel", "arbitrary")),
        name=name,
    )(x, w3, cos, sin_s)


def _swiglu_body(x_ref, wg_ref, wu_ref, o_ref, accg_ref, accu_ref, *, nk):
    k = pl.program_id(2)

    @pl.when(k == 0)
    def _init():
        accg_ref[...] = jnp.zeros_like(accg_ref)
        accu_ref[...] = jnp.zeros_like(accu_ref)

    xb = x_ref[...].astype(BF16)
    accg_ref[...] += jnp.dot(xb, wg_ref[...].astype(BF16), preferred_element_type=F32)
    accu_ref[...] += jnp.dot(xb, wu_ref[...].astype(BF16), preferred_element_type=F32)

    @pl.when(k == nk - 1)
    def _fin():
        g = accg_ref[...]
        o_ref[...] = (g * jax.nn.sigmoid(g) * accu_ref[...]).astype(o_ref.dtype)


def _swiglu_in(x, w3, li, *, tm, tn, tk, name="swiglu_in"):
    m, kdim = x.shape
    _, _, n2 = w3.shape
    hid = n2 // 2
    assert m % tm == 0 and hid % tn == 0 and kdim % tk == 0
    nk = kdim // tk
    nt = hid // tn
    return pl.pallas_call(
        functools.partial(_swiglu_body, nk=nk),
        grid=(m // tm, nt, nk),
        in_specs=[pl.BlockSpec((tm, tk), lambda i, j, k: (i, k)),
                  pl.BlockSpec((None, tk, tn), lambda i, j, k: (li, k, j)),
                  pl.BlockSpec((None, tk, tn), lambda i, j, k: (li, k, j + nt))],
        out_specs=pl.BlockSpec((tm, tn), lambda i, j, k: (i, j)),
        out_shape=jax.ShapeDtypeStruct((m, hid), BF16),
        scratch_shapes=[pltpu.VMEM((tm, tn), F32), pltpu.VMEM((tm, tn), F32)],
        compiler_params=_cparams(("parallel", "parallel", "arbitrary")),
        name=name,
    )(x, w3, w3)


def _ln_body(x_ref, f_ref, g_ref, b_ref, o_ref, ob_ref, *, alpha):
    z = alpha * x_ref[...] + f_ref[...]
    mu = jnp.mean(z, -1, keepdims=True)
    zc = z - mu
    var = jnp.mean(zc * zc, -1, keepdims=True)
    y = zc * lax.rsqrt(var + LN_EPS) * g_ref[...] + b_ref[...]
    o_ref[...] = y
    ob_ref[...] = y.astype(BF16)


def _res_ln(x, f, g, b, *, alpha, tm, name="res_ln"):
    m, d = x.shape
    assert m % tm == 0
    row = pl.BlockSpec((tm, d), lambda i: (i, 0))
    vec = pl.BlockSpec((1, d), lambda i: (0, 0))
    return pl.pallas_call(
        functools.partial(_ln_body, alpha=alpha),
        grid=(m // tm,),
        in_specs=[row, row, vec, vec],
        out_specs=[row, row],
        out_shape=[jax.ShapeDtypeStruct((m, d), F32), jax.ShapeDtypeStruct((m, d), BF16)],
        compiler_params=_cparams(("parallel",)),
        name=name,
    )(x, f, g.reshape(1, d), b.reshape(1, d))


def _combine_ln_body(x_ref, ya_ref, yb_ref, gt_ref, g_ref, b_ref, o_ref, *, alpha):
    gt = gt_ref[...]
    f = ya_ref[...] * gt[:, 0:1] + yb_ref[...] * gt[:, 1:2]
    z = alpha * x_ref[...] + f
    mu = jnp.mean(z, -1, keepdims=True)
    zc = z - mu
    var = jnp.mean(zc * zc, -1, keepdims=True)
    o_ref[...] = zc * lax.rsqrt(var + LN_EPS) * g_ref[...] + b_ref[...]


def _combine_ln(x, ya, yb, gates, g, b, *, alpha, tm, name="moe_combine_ln"):
    m, d = x.shape
    assert m % tm == 0
    row = pl.BlockSpec((tm, d), lambda i: (i, 0))
    vec = pl.BlockSpec((1, d), lambda i: (0, 0))
    return pl.pallas_call(
        functools.partial(_combine_ln_body, alpha=alpha),
        grid=(m // tm,),
        in_specs=[row, row, row, pl.BlockSpec((tm, gates.shape[1]), lambda i: (i, 0)), vec, vec],
        out_specs=row,
        out_shape=jax.ShapeDtypeStruct((m, d), F32),
        compiler_params=_cparams(("parallel",)),
        name=name,
    )(x, ya, yb, gates, g.reshape(1, d), b.reshape(1, d))


def _wkv_body(r_ref, w_ref, k_ref, v_ref, a_ref, b_ref, s0_ref, y_ref, s_ref, *, tc, n):
    t_blk = pl.program_id(1)

    @pl.when(t_blk == 0)
    def _init():
        s_ref[...] = s0_ref[...]

    def row(ref, t, j):
        return ref[t, pl.ds(j, 1), :]

    def step(t, carry):
        sa = jnp.zeros((n, LANES), F32)
        for j in range(n):
            sa = sa + s_ref[j] * row(a_ref, t, j)
        v_t = v_ref[t]
        y = jnp.zeros((n, LANES), F32)
        for j in range(n):
            s_new = s_ref[j] * row(w_ref, t, j) + sa * row(b_ref, t, j) + v_t * row(k_ref, t, j)
            s_ref[j] = s_new
            y = y + s_new * row(r_ref, t, j)
        y_ref[t] = y
        return carry

    lax.fori_loop(0, tc, step, 0)


def _wkv_scan(r, w, k, v, a, b, s0, *, tc, name="wkv_scan"):
    t_len, n, lanes = r.shape
    assert lanes % LANES == 0 and t_len % tc == 0
    seq = pl.BlockSpec((tc, n, LANES), lambda l, t: (t, 0, l))
    st = pl.BlockSpec((n, n, LANES), lambda l, t: (0, 0, l))
    return pl.pallas_call(
        functools.partial(_wkv_body, tc=tc, n=n),
        grid=(lanes // LANES, t_len // tc),
        in_specs=[seq] * 6 + [st],
        out_specs=[seq, st],
        out_shape=[jax.ShapeDtypeStruct((t_len, n, lanes), F32),
                   jax.ShapeDtypeStruct((n, n, lanes), F32)],
        compiler_params=_cparams(("parallel", "arbitrary")),
        name=name,
    )(r, w, k, v, a, b, s0)


CHUNK = 128
SUB = 32


def _head_normalize(x_t, hl, n):
    x3 = x_t.reshape(hl, n, CHUNK)
    nrm = jnp.sqrt(jnp.sum(x3 * x3, axis=1, keepdims=True))
    return (x3 / jnp.maximum(nrm, 1e-12)).reshape(hl * n, CHUNK)


def _to_scan_body(*refs, mode, nb, hl, n):
    n_in = {"copy": 1, "k2": 2, "kk": 1, "kb": 2}[mode]
    tiles = [refs[i * nb:(i + 1) * nb] for i in range(n_in)]
    pos = n_in * nb
    par_ref = None
    if mode != "copy":
        par_ref = refs[pos]
        pos += 1
    o_ref, sc_ref = refs[pos], refs[pos + 1]

    for b in range(nb):
        if mode == "copy":
            res = tiles[0][b][...].T
        elif mode == "k2":
            k, a = tiles[0][b][...], tiles[1][b][...]
            res = (k * (1.0 + (a - 1.0) * par_ref[...])).T
        elif mode == "kk":
            res = _head_normalize((tiles[0][b][...] * par_ref[...]).T, hl, n)
        else:
            kk = _head_normalize((tiles[0][b][...] * par_ref[...]).T, hl, n)
            res = kk * tiles[1][b][...].T
        sc_ref[b] = res

    for j in range(n):
        slab = jnp.concatenate([sc_ref.at[b][pl.ds(j, hl, stride=n), :] for b in range(nb)], axis=0)
        slab_t = slab.T
        for s in range(CHUNK // SUB):
            o_ref[s, pl.ds(j * SUB, SUB), :] = slab_t[s * SUB:(s + 1) * SUB, :]


def _to_scan(inputs, par, *, mode, nb, seq, heads, n, name):
    hl = LANES // nb
    width = hl * n
    n_hh = heads // hl
    n_c = seq // CHUNK
    in_specs, args = [], []
    for x in inputs:
        for b in range(nb):
            in_specs.append(pl.BlockSpec((CHUNK, width), lambda hh, c, b=b: (b * n_c + c, hh)))
            args.append(x)
    if par is not None:
        in_specs.append(pl.BlockSpec((1, width), lambda hh, c: (0, hh)))
        args.append(par.reshape(1, heads * n))
    return pl.pallas_call(
        functools.partial(_to_scan_body, mode=mode, nb=nb, hl=hl, n=n),
        grid=(n_hh, n_c),
        in_specs=in_specs,
        out_specs=pl.BlockSpec((CHUNK // SUB, n * SUB, LANES), lambda hh, c: (c, 0, hh)),
        out_shape=jax.ShapeDtypeStruct((seq // SUB, n * SUB, n_hh * LANES), F32),
        scratch_shapes=[pltpu.VMEM((nb, width, CHUNK), F32)],
        compiler_params=_cparams(("parallel", "parallel")),
        name=name,
    )(*args)


def _wkv_sl_body(r_ref, w_ref, k_ref, v_ref, kk_ref, b_ref, y_ref, s_ref, *, n):
    @pl.when(pl.program_id(1) == 0)
    def _init():
        s_ref[...] = jnp.zeros_like(s_ref)

    def row(ref, t, j):
        return ref[pl.ds(j * SUB + t, 1), :]

    def step(t, skk):
        t_next = jnp.minimum(t + 1, SUB - 1)
        v_t = v_ref[pl.ds(t, n, stride=SUB), :]
        y = jnp.zeros((n, LANES), F32)
        skk_next = jnp.zeros((n, LANES), F32)
        for j in range(n):
            s_new = s_ref[j] * row(w_ref, t, j) - skk * row(b_ref, t, j) + v_t * row(k_ref, t, j)
            s_ref[j] = s_new
            y = y + s_new * row(r_ref, t, j)
            skk_next = skk_next + s_new * row(kk_ref, t_next, j)
        y_ref[pl.ds(t, n, stride=SUB), :] = y
        return skk_next

    skk0 = jnp.zeros((n, LANES), F32)
    for j in range(n):
        skk0 = skk0 + s_ref[j] * row(kk_ref, 0, j)
    lax.fori_loop(0, SUB, step, skk0)


def _wkv_scan_sl(r, w, k, v, kk, b, *, n, name="wkv_scan_prompt"):
    n_blk, rows, lanes = r.shape
    seq = pl.BlockSpec((None, rows, LANES), lambda l, t: (t, 0, l))
    st = pl.BlockSpec((n, n, LANES), lambda l, t: (0, 0, l))
    return pl.pallas_call(
        functools.partial(_wkv_sl_body, n=n),
        grid=(lanes // LANES, n_blk),
        in_specs=[seq] * 6,
        out_specs=[seq, st],
        out_shape=[jax.ShapeDtypeStruct((n_blk, rows, lanes), F32),
                   jax.ShapeDtypeStruct((n, n, lanes), F32)],
        compiler_params=_cparams(("parallel", "arbitrary")),
        name=name,
    )(r, w, k, v, kk, b)


def _from_scan_body(y_ref, r_ref, k_ref, v_ref, rk_ref, lg_ref, lb_ref, g_ref, o_ref, z_ref, sc_ref,
                    *, nb, hl, n, eps):
    b = pl.program_id(2)

    @pl.when(b == 0)
    def _fill():
        for s in range(CHUNK // SUB):
            def blk(ref, j, s=s):
                return ref[s, pl.ds(j * SUB, SUB), :]

            coef = jnp.zeros((SUB, LANES), F32)
            tot = jnp.zeros((SUB, LANES), F32)
            for j in range(n):
                coef = coef + blk(r_ref, j) * blk(k_ref, j) * rk_ref[pl.ds(j, 1), :]
                tot = tot + blk(y_ref, j)
            mu = tot / n
            var = jnp.zeros((SUB, LANES), F32)
            for j in range(n):
                d = blk(y_ref, j) - mu
                var = var + d * d
            inv = lax.rsqrt(var / n + eps)
            for j in range(n):
                z = (blk(y_ref, j) - mu) * inv * lg_ref[pl.ds(j, 1), :] + lb_ref[pl.ds(j, 1), :]
                z_ref[j, pl.ds(s * SUB, SUB), :] = z + coef * blk(v_ref, j)
        for j in range(n):
            slab_t = z_ref[j].T
            for bb in range(nb):
                sc_ref.at[bb][pl.ds(j, hl, stride=n), :] = slab_t[bb * hl:(bb + 1) * hl, :]

    o_ref[...] = (sc_ref[b].T * g_ref[...]).astype(o_ref.dtype)


def _from_scan(y, r, k2, v, rk_t, lg_t, lb_t, g, *, nb, seq, heads, n, eps, name="rwkv_post"):
    hl = LANES // nb
    width = hl * n
    n_hh = heads // hl
    n_c = seq // CHUNK
    sl = pl.BlockSpec((CHUNK // SUB, n * SUB, LANES), lambda hh, c, b: (c, 0, hh))
    par = pl.BlockSpec((n, LANES), lambda hh, c, b: (0, hh))
    tok = pl.BlockSpec((CHUNK, width), lambda hh, c, b: (b * n_c + c, hh))
    return pl.pallas_call(
        functools.partial(_from_scan_body, nb=nb, hl=hl, n=n, eps=eps),
        grid=(n_hh, n_c, nb),
        in_specs=[sl, sl, sl, sl, par, par, par, tok],
        out_specs=tok,
        out_shape=jax.ShapeDtypeStruct((nb * seq, heads * n), BF16),
        scratch_shapes=[pltpu.VMEM((n, CHUNK, LANES), F32), pltpu.VMEM((nb, width, CHUNK), F32)],
        compiler_params=_cparams(("parallel", "parallel", "arbitrary")),
        name=name,
    )(y, r, k2, v, rk_t, lg_t, lb_t, g)


def _head_param_lanes(p, nb, heads, n):
    hl = LANES // nb
    t = p.reshape(heads // hl, 1, hl, n)
    t = jnp.broadcast_to(t, (heads // hl, nb, hl, n))
    return jnp.transpose(t, (3, 0, 1, 2)).reshape(n, (heads // hl) * LANES).astype(F32)


SPAN = 128
ATTN_UNROLL = 3


def _attn_block(q, k, v, causal_own, scale):
    s = lax.dot_general(q.astype(BF16), k.astype(BF16), (((1,), (1,)), ((), ())),
                        preferred_element_type=F32) * scale
    qi = lax.broadcasted_iota(jnp.int32, s.shape, 0)
    ki = lax.broadcasted_iota(jnp.int32, s.shape, 1)
    if causal_own:
        mask = ki <= qi
    else:
        mask = (ki >= qi) & (ki <= qi + SPAN)
    s = jnp.where(mask, s, -jnp.inf)
    m = jnp.max(s, axis=1, keepdims=True)
    p = jnp.exp(s - m)
    l = jnp.sum(p, axis=1, keepdims=True)
    o = jnp.dot(p.astype(BF16), v.astype(BF16), preferred_element_type=F32)
    return o, m, l


def _dilated_attn_body(*refs, dils, seq, scale):
    ng = len(dils)
    q_refs, k_refs, v_refs = refs[:ng], refs[ng:2 * ng], refs[2 * ng:3 * ng]
    o_ref = refs[3 * ng]
    scratch = refs[3 * ng + 1:]
    acc_refs, m_refs, l_refs = scratch[:ng], scratch[ng:2 * ng], scratch[2 * ng:]

    for g, dil in enumerate(dils):
        q_ref, k_ref, v_ref = q_refs[g], k_refs[g], v_refs[g]
        acc_ref, m_ref, l_ref = acc_refs[g], m_refs[g], l_refs[g]
        nblk = seq // (dil * SPAN)

        def rows(start, size, dil=dil):
            if dil == 1:
                return pl.ds(start, size)
            return pl.ds(start, size, stride=dil)

        def put(sl, res, acc_ref=acc_ref, m_ref=m_ref, l_ref=l_ref):
            o, m, l = res
            acc_ref[sl, :] = o
            m_ref[sl, :] = m
            l_ref[sl, :] = l

        for r in range(dil):
            sl0 = rows(r, SPAN)
            put(sl0, _attn_block(q_ref[sl0, :], k_ref[sl0, :], v_ref[sl0, :], True, scale))

            def body(nb, carry, r=r, dil=dil, rows=rows, put=put, q_ref=q_ref, k_ref=k_ref, v_ref=v_ref):
                q_sl = rows(r + dil * SPAN * nb, SPAN)
                kv_sl = rows(r + dil * SPAN * (nb - 1), 2 * SPAN)
                put(q_sl, _attn_block(q_ref[q_sl, :], k_ref[kv_sl, :], v_ref[kv_sl, :], False, scale))
                return carry

            if nblk > 1:
                lax.fori_loop(1, nblk, body, 0, unroll=ATTN_UNROLL)

    m_all = m_refs[0][...]
    for g in range(1, ng):
        m_all = jnp.maximum(m_all, m_refs[g][...])
    num = jnp.zeros(acc_refs[0].shape, F32)
    den = jnp.zeros(m_all.shape, F32)
    for g in range(ng):
        e = jnp.exp(m_refs[g][...] - m_all)
        num = num + e * acc_refs[g][...]
        den = den + e * l_refs[g][...]
    o_ref[...] = (num / den).astype(o_ref.dtype)


def _dilated_attn(q_hm, k_hm, v_hm, *, batch, seq, heads, dils, v_head0=0, name="dilated_attn"):
    e = q_hm.shape[-1]
    ng = len(dils)

    def spec(g, h0=0):
        return pl.BlockSpec((None, seq, e), lambda b, h, g=g: (h0 + g * heads + h, b, 0))

    specs = [spec(g) for g in range(ng)]
    return pl.pallas_call(
        functools.partial(_dilated_attn_body, dils=dils, seq=seq, scale=e ** -0.5),
        grid=(batch, heads),
        in_specs=specs * 2 + [spec(g, v_head0) for g in range(ng)],
        out_specs=pl.BlockSpec((seq, e), lambda b, h: (b, h)),
        out_shape=jax.ShapeDtypeStruct((batch * seq, heads * e), BF16),
        scratch_shapes=([pltpu.VMEM((seq, e), F32)] * ng + [pltpu.VMEM((seq, 1), F32)] * (2 * ng)),
        compiler_params=_cparams(("parallel", "parallel")),
        name=name,
    )(*([q_hm] * ng + [k_hm] * ng + [v_hm] * ng))


def _sample_attn_body(q_ref, kn_ref, vn_ref, *rest, ng, heads, scale):
    cache_refs, o_ref = rest[:ng], rest[ng]
    for h in range(heads):
        parts = []
        for g in range(ng):
            row = g * heads + h
            q = q_ref[pl.ds(row, 1), :]
            kc = cache_refs[g][:, 0, h, :]
            vc = cache_refs[g][:, 1, h, :]
            kn, vn = kn_ref[pl.ds(row, 1), :], vn_ref[pl.ds(row, 1), :]
            qb = q.astype(BF16)
            s = lax.dot_general(qb, kc.astype(BF16), (((1,), (1,)), ((), ())),
                                preferred_element_type=F32) * scale
            s_n = jnp.sum(qb.astype(F32) * kn.astype(BF16).astype(F32), axis=1, keepdims=True) * scale
            m = jnp.maximum(jnp.max(s, axis=1, keepdims=True), s_n)
            p, p_n = jnp.exp(s - m), jnp.exp(s_n - m)
            l = jnp.sum(p, axis=1, keepdims=True) + p_n
            o = jnp.dot(p.astype(BF16), vc.astype(BF16), preferred_element_type=F32)
            o = o + p_n.astype(BF16).astype(F32) * vn.astype(BF16).astype(F32)
            parts.append((o, m, l))
        m_all = parts[0][1]
        for _, m, _ in parts[1:]:
            m_all = jnp.maximum(m_all, m)
        num = jnp.zeros_like(parts[0][0])
        den = jnp.zeros_like(m_all)
        for o, m, l in parts:
            e = jnp.exp(m - m_all)
            num = num + e * o
            den = den + e * l
        o_ref[pl.ds(h, 1), :] = (num / den).astype(o_ref.dtype)


def _sample_attn(q_s, k_new, v_new, caches, dils, *, heads, name="sample_attn"):
    bs, _, e = q_s.shape
    ng = len(dils)
    tok = pl.BlockSpec((None, ng * heads, e), lambda b: (b, 0, 0))
    in_specs, args = [tok, tok, tok], [q_s, k_new, v_new]
    for c, dil in zip(caches, dils):
        w = c.shape[1]
        assert w == SPAN * dil
        args.append(c.reshape(bs, SPAN, dil, 2, heads, e))
        in_specs.append(pl.BlockSpec((None, SPAN, None, 2, heads, e), lambda b: (b, 0, 0, 0, 0, 0)))
    return pl.pallas_call(
        functools.partial(_sample_attn_body, ng=ng, heads=heads, scale=e ** -0.5),
        grid=(bs,),
        in_specs=in_specs,
        out_specs=pl.BlockSpec((None, heads, e), lambda b: (b, 0, 0)),
        out_shape=jax.ShapeDtypeStruct((bs, heads, e), BF16),
        compiler_params=_cparams(("parallel",)),
        name=name,
    )(*args)


def _router_body(x_ref, w_ref, o_ref):
    o_ref[...] = jnp.dot(x_ref[...], w_ref[...], precision=lax.Precision.HIGHEST,
                         preferred_element_type=F32)


def _router_logits(x, router, *, tm, name="router_logits"):
    m, d = x.shape
    n_exp = router.shape[1]
    w = jnp.zeros((d, LANES), F32).at[:, :n_exp].set(router.astype(F32))
    out = pl.pallas_call(
        _router_body,
        grid=(m // tm,),
        in_specs=[pl.BlockSpec((tm, d), lambda i: (i, 0)), pl.BlockSpec((d, LANES), lambda i: (0, 0))],
        out_specs=pl.BlockSpec((tm, LANES), lambda i: (i, 0)),
        out_shape=jax.ShapeDtypeStruct((m, LANES), F32),
        compiler_params=_cparams(("parallel",)),
        name=name,
    )(x, w)
    return out[:, :n_exp]


def _moe_in_body(be_ref, nu_ref, x_ref, wg_ref, wu_ref, o_ref, wgb_ref, wub_ref):
    r = pl.program_id(1)
    changed = (r == 0) | (be_ref[r] != be_ref[jnp.maximum(r - 1, 0)])

    @pl.when(changed)
    def _cast():
        wgb_ref[...] = wg_ref[...].astype(BF16)
        wub_ref[...] = wu_ref[...].astype(BF16)

    @pl.when(r < nu_ref[0])
    def _compute():
        x = x_ref[...]
        g = jnp.dot(x, wgb_ref[...], preferred_element_type=F32)
        u = jnp.dot(x, wub_ref[...], preferred_element_type=F32)
        o_ref[...] = (g * jax.nn.sigmoid(g) * u).astype(o_ref.dtype)

    @pl.when(r >= nu_ref[0])
    def _skip():
        o_ref[...] = jnp.zeros_like(o_ref)


def _moe_in(x_buf, w_in, blk_e, n_used, *, tn, name="moe_swiglu_in"):
    rows, d = x_buf.shape
    n_exp, _, n2 = w_in.shape
    hid = n2 // 2
    nt = hid // tn
    rb = rows // MOE_ROWS

    def x_map(j, r, be, nu):
        return (jnp.minimum(r, nu[0] - 1), 0)

    grid_spec = pltpu.PrefetchScalarGridSpec(
        num_scalar_prefetch=2,
        grid=(nt, rb),
        in_specs=[pl.BlockSpec((MOE_ROWS, d), x_map),
                  pl.BlockSpec((None, d, tn), lambda j, r, be, nu: (be[r], 0, j)),
                  pl.BlockSpec((None, d, tn), lambda j, r, be, nu: (be[r], 0, j + nt))],
        out_specs=pl.BlockSpec((MOE_ROWS, tn), lambda j, r, be, nu: (r, j)),
        scratch_shapes=[pltpu.VMEM((d, tn), BF16), pltpu.VMEM((d, tn), BF16)],
    )
    return pl.pallas_call(
        _moe_in_body,
        grid_spec=grid_spec,
        out_shape=jax.ShapeDtypeStruct((rows, hid), BF16),
        compiler_params=_cparams(("arbitrary", "arbitrary")),
        name=name,
    )(blk_e, n_used, x_buf, w_in, w_in)


def _moe_out_body(be_ref, nu_ref, h_ref, w_ref, o_ref, acc_ref, *, nk):
    r = pl.program_id(0)
    k = pl.program_id(1)

    @pl.when(k == 0)
    def _init():
        acc_ref[...] = jnp.zeros_like(acc_ref)

    @pl.when(r < nu_ref[0])
    def _compute():
        acc_ref[...] += jnp.dot(h_ref[...], w_ref[...].astype(BF16), preferred_element_type=F32)

    @pl.when(k == nk - 1)
    def _fin():
        o_ref[...] = acc_ref[...]


def _moe_out(h_buf, w_out, blk_e, n_used, *, tk, name="moe_out"):
    rows, hid = h_buf.shape
    n_exp, _, d = w_out.shape
    rb = rows // MOE_ROWS
    nk = hid // tk

    def k_eff(r, k, nu):
        return jnp.where(r < nu[0], k, nk - 1)

    grid_spec = pltpu.PrefetchScalarGridSpec(
        num_scalar_prefetch=2,
        grid=(rb, nk),
        in_specs=[pl.BlockSpec((MOE_ROWS, tk),
                               lambda r, k, be, nu: (jnp.minimum(r, nu[0] - 1), k_eff(r, k, nu))),
                  pl.BlockSpec((None, tk, d), lambda r, k, be, nu: (be[r], k_eff(r, k, nu), 0))],
        out_specs=pl.BlockSpec((MOE_ROWS, d), lambda r, k, be, nu: (r, 0)),
        scratch_shapes=[pltpu.VMEM((MOE_ROWS, d), F32)],
    )
    return pl.pallas_call(
        functools.partial(_moe_out_body, nk=nk),
        grid_spec=grid_spec,
        out_shape=jax.ShapeDtypeStruct((rows, d), F32),
        compiler_params=_cparams(("arbitrary", "arbitrary")),
        name=name,
    )(blk_e, n_used, h_buf, w_out)


def _moe(x_f32, n_tok, router, w_in, w_out):
    rows, d = x_f32.shape
    n_exp = router.shape[1]
    top_k = 2
    logits = _router_logits(x_f32, router, tm=520)[:n_tok]
    top_v, top_i = lax.top_k(logits, top_k)
    gates = jax.nn.softmax(top_v, axis=-1)
    n_asg = n_tok * top_k
    e = top_i.reshape(n_asg).astype(jnp.int32)
    tok = jnp.repeat(jnp.arange(n_tok, dtype=jnp.int32), top_k)
    order = jnp.argsort(e)
    e_s, tok_s = e[order], tok[order]
    counts = jnp.bincount(e, length=n_exp).astype(jnp.int32)
    starts = jnp.cumsum(counts) - counts
    padded = -(-counts // MOE_ROWS) * MOE_ROWS
    padded_ends = jnp.cumsum(padded)
    padded_starts = padded_ends - padded
    dest = (padded_starts[e_s] + jnp.arange(n_asg, dtype=jnp.int32) - starts[e_s]).astype(jnp.int32)
    n_blk = -(-n_asg // MOE_ROWS) + n_exp
    src_tok = jnp.zeros((n_blk * MOE_ROWS,), jnp.int32).at[dest].set(tok_s)
    pos = jnp.zeros((n_asg,), jnp.int32).at[order].set(dest).reshape(n_tok, top_k)
    blk_e = jnp.minimum(jnp.searchsorted(padded_ends, jnp.arange(n_blk, dtype=jnp.int32) * MOE_ROWS,
                                         side='right'), n_exp - 1).astype(jnp.int32)
    n_used = (padded_ends[-1:] // MOE_ROWS).astype(jnp.int32)

    x_buf = x_f32[src_tok].astype(BF16)
    h_buf = _moe_in(x_buf, w_in, blk_e, n_used, tn=512)
    y_buf = _moe_out(h_buf, w_out, blk_e, n_used, tk=512)
    pos = jnp.zeros((rows, top_k), jnp.int32).at[:n_tok].set(pos)
    gates = jnp.zeros((rows, top_k), F32).at[:n_tok].set(gates)
    return y_buf[pos[:, 0]], y_buf[pos[:, 1]], gates


def _rope_tables(pos, hd):
    inv = ROPE_THETA ** (-jnp.arange(0, hd, 2, dtype=F32) / hd)
    ang = pos.astype(F32)[:, None] * inv[None, :]
    ang = jnp.concatenate([ang, ang], -1)
    return jnp.cos(ang), jnp.sin(ang)


def _softplus(z):
    return jnp.maximum(z, 0.0) + jnp.log(1.0 + jnp.exp(-jnp.abs(z)))


def _decay_epilogue(y):
    return jnp.exp(-jnp.exp(-_softplus(-y) - 0.5))


def kernel(x_prompt, x_sample, state_wkv, state_shift, cache_kv_w128, cache_kv_w512, cache_kv_w2048,
           ln_g, ln_b, rwkv_mix, rwkv_w_rkv, rwkv_w0, rwkv_w1, rwkv_w2, rwkv_a0, rwkv_a1, rwkv_a2,
           rwkv_g1, rwkv_g2, rwkv_k_k, rwkv_k_a, rwkv_r_k, rwkv_lnx_g, rwkv_lnx_b, rwkv_w_out,
           attn_w_kv, attn_w_q, attn_w_out, ffn_w_in, ffn_w_out, moe_router, moe_w_in, moe_w_out):
    bp, seq, d = x_prompt.shape
    bs, seq_s, _ = x_sample.shape
    depth = ln_g.shape[0]
    assert depth == 2 and seq_s == 1
    heads_r, hd_r = rwkv_r_k.shape[1], rwkv_r_k.shape[2]
    past_len = PAST_LEN
    groups = ((128, 1), (512, 4), (2048, 16))
    ng = len(groups)
    hd_a = 128
    heads_a = attn_w_out.shape[1] // hd_a
    alpha = (2.0 * depth) ** 0.25
    lnx_eps = 1e-5 * hd_r

    n_p = bp * seq
    n_tok = n_p + bs
    tm = 2080
    rows = -(-n_tok // tm) * tm

    def pad_rows(a):
        return jnp.concatenate([a, jnp.zeros((rows - a.shape[0],) + a.shape[1:], a.dtype)], 0)

    x0 = pad_rows(jnp.concatenate([x_prompt.reshape(n_p, d), x_sample.reshape(bs, d)], 0))
    prev_p = jnp.concatenate([jnp.zeros((bp, 1, d), F32), x_prompt[:, :-1]], 1).reshape(n_p, d)
    x_prev = pad_rows(jnp.concatenate([prev_p, state_shift[0]], 0))
    xx = x_prev - x0
    xs = [(x0 + xx * rwkv_mix[0, c]).astype(BF16) for c in range(6)]

    w_rkv = rwkv_w_rkv.reshape(3, d, d)
    mm = functools.partial(_matmul, tm=tm)
    r = mm(xs[0], w_rkv, 0, tn=1024, tk=1024, name="rwkv_r")
    k = mm(xs[1], w_rkv, 1, tn=1024, tk=1024, name="rwkv_k")
    v = mm(xs[2], w_rkv, 2, tn=1024, tk=1024, name="rwkv_v")
    lw, la, lg = rwkv_w1.shape[2], rwkv_a1.shape[2], rwkv_g1.shape[2]
    hw = mm(xs[3], rwkv_w1, 0, tn=lw, tk=1024, out_dtype=BF16, epilogue=jnp.tanh, name="rwkv_w1")
    decay = mm(hw, rwkv_w2, 0, tn=1024, tk=lw, bias=rwkv_w0[0], epilogue=_decay_epilogue, name="rwkv_w2")
    ha = mm(xs[4], rwkv_a1, 0, tn=la, tk=1024, out_dtype=BF16, name="rwkv_a1")
    a = mm(ha, rwkv_a2, 0, tn=1024, tk=la, bias=rwkv_a0[0], epilogue=jax.nn.sigmoid, name="rwkv_a2")
    hg = mm(xs[5], rwkv_g1, 0, tn=lg, tk=1024, out_dtype=BF16, epilogue=jax.nn.sigmoid, name="rwkv_g1")
    g = mm(hg, rwkv_g2, 0, tn=1024, tk=lg, name="rwkv_g2")

    hl = LANES // bp
    assert LANES % bp == 0 and heads_r % hl == 0 and seq % CHUNK == 0
    kw = dict(nb=bp, seq=seq, heads=heads_r, n=hd_r)
    r_sl = _to_scan([r], None, mode="copy", name="sl_r", **kw)
    w_sl = _to_scan([decay], None, mode="copy", name="sl_w", **kw)
    v_sl = _to_scan([v], None, mode="copy", name="sl_v", **kw)
    k_sl = _to_scan([k, a], rwkv_k_a[0], mode="k2", name="sl_k", **kw)
    kk_sl = _to_scan([k], rwkv_k_k[0], mode="kk", name="sl_kk", **kw)
    b_sl = _to_scan([k, a], rwkv_k_k[0], mode="kb", name="sl_b", **kw)
    y_sl, s_p = _wkv_scan_sl(r_sl, w_sl, k_sl, v_sl, kk_sl, b_sl, n=hd_r)
    tab = functools.partial(_head_param_lanes, nb=bp, heads=heads_r, n=hd_r)
    y_p = _from_scan(y_sl, r_sl, k_sl, v_sl, tab(rwkv_r_k.reshape(-1)), tab(rwkv_lnx_g[0]),
                     tab(rwkv_lnx_b[0]), g, eps=lnx_eps, **kw)
    s_p = s_p.reshape(hd_r, hd_r, heads_r // hl, bp, hl)
    prompt_wkv = jnp.transpose(s_p, (3, 2, 4, 1, 0)).reshape(1, bp, heads_r, hd_r, hd_r).astype(state_wkv.dtype)

    def hs(t):
        return t[n_p:n_tok].reshape(bs, heads_r, hd_r)

    kk_s = hs(k) * rwkv_k_k[0].reshape(heads_r, hd_r)
    kk_s = kk_s / jnp.maximum(jnp.linalg.norm(kk_s, axis=-1, keepdims=True), 1e-12)
    k2_s = hs(k) * (1.0 + (hs(a) - 1.0) * rwkv_k_a[0].reshape(heads_r, hd_r))
    bonus_s = jnp.sum(hs(r) * k2_s * rwkv_r_k[0], -1, keepdims=True) * hs(v)

    def lanes_s(t):
        return jnp.transpose(t, (2, 0, 1)).reshape(1, hd_r, bs * heads_r)

    scan_s = (hs(r), hs(decay), k2_s, hs(v), -kk_s, kk_s * hs(a))
    s0_s = jnp.transpose(state_wkv[0].astype(F32), (3, 2, 0, 1)).reshape(hd_r, hd_r, bs * heads_r)
    y_s, s_s = _wkv_scan(*[lanes_s(t) for t in scan_s], s0_s, tc=1, name="wkv_scan_sample")
    y_s = jnp.transpose(y_s.reshape(hd_r, bs, heads_r), (1, 2, 0))
    sample_wkv = jnp.transpose(s_s.reshape(hd_r, hd_r, bs, heads_r), (2, 3, 1, 0))[None].astype(state_wkv.dtype)
    mu = jnp.mean(y_s, -1, keepdims=True)
    var = jnp.mean(jnp.square(y_s - mu), -1, keepdims=True)
    y_s = ((y_s - mu) * lax.rsqrt(var + lnx_eps)).reshape(bs, d) * rwkv_lnx_g[0] + rwkv_lnx_b[0]
    y_s = ((y_s + bonus_s.reshape(bs, d)) * g[n_p:n_tok]).astype(BF16)

    y = pad_rows(jnp.concatenate([y_p, y_s], 0))
    mix0 = mm(y, rwkv_w_out, 0, tn=1024, tk=1024, name="rwkv_out")
    x1, x1b = _res_ln(x0, mix0, ln_g[0, 0], ln_b[0, 0], alpha=alpha, tm=208, name="ln_0a")

    h = _swiglu_in(x1b, ffn_w_in, 0, tm=tm, tn=256, tk=2048, name="ffn_in")
    f0 = mm(h, ffn_w_out, 0, tn=1024, tk=1024, name="ffn_out")
    x2, x2b = _res_ln(x1, f0, ln_g[0, 1], ln_b[0, 1], alpha=alpha, tm=208, name="ln_0b")

    pos = jnp.concatenate([jnp.tile(jnp.arange(seq, dtype=jnp.int32), bp),
                           jnp.full((bs,), past_len, jnp.int32),
                           jnp.zeros((rows - n_tok,), jnp.int32)])
    cos, sin = _rope_tables(pos, hd_a)
    half_sign = jnp.concatenate([-jnp.ones((hd_a // 2,), F32), jnp.ones((hd_a // 2,), F32)])
    sin_s = sin * half_sign
    n_qh = ng * heads_a
    tn_h = 1024
    kv_hm, kv_tm = _matmul_heads(x2b, attn_w_kv.reshape(1, d, -1), 0, cos, sin_s, n_rope=n_qh * hd_a // tn_h,
                                 tm=tm // 2, tn=tn_h, tk=1024, hd=hd_a, token_major=True, name="attn_kv")
    (q_hm,) = _matmul_heads(x2b, attn_w_q, 0, cos, sin_s, n_rope=n_qh * hd_a // tn_h,
                            tm=tm, tn=tn_h, tk=1024, hd=hd_a, token_major=False, name="attn_q")
    dils = tuple(dl for _, dl in groups)
    att_p = _dilated_attn(q_hm, kv_hm, kv_hm, batch=bp, seq=seq, heads=heads_a, dils=dils, v_head0=n_qh)

    caches = (cache_kv_w128, cache_kv_w512, cache_kv_w2048)
    q_s = jnp.transpose(q_hm[:, n_p:n_tok], (1, 0, 2))
    kv_s = kv_tm[n_p:n_tok].reshape(bs, 2, n_qh, hd_a)
    att_s = _sample_attn(q_s, kv_s[:, 0], kv_s[:, 1], caches, dils, heads=heads_a)
    att = pad_rows(jnp.concatenate([att_p, att_s.reshape(bs, heads_a * hd_a)], 0))
    mix1 = mm(att, attn_w_out, 0, tn=1024, tk=1024, name="attn_out")
    x3, x3b = _res_ln(x2, mix1, ln_g[1, 0], ln_b[1, 0], alpha=alpha, tm=208, name="ln_1a")

    n_exp = moe_router.shape[-1]
    ya, yb, gates = _moe(x3, n_tok, moe_router.reshape(d, n_exp), moe_w_in.reshape(n_exp, d, -1),
                         moe_w_out.reshape(n_exp, -1, d))
    x4 = _combine_ln(x3, ya, yb, gates, ln_g[1, 1], ln_b[1, 1], alpha=alpha, tm=208)

    y_prompt = x4[:n_p].reshape(bp, seq, d)
    y_sample = x4[n_p:n_tok].reshape(bs, 1, d)
    prompt_shift = x_prompt[:, -1][None]
    sample_shift = x_sample[:, -1][None]
    prompt_kv, sample_kv = [], []
    kv_p = kv_tm[:n_p].reshape(bp, seq, 2, ng, heads_a, hd_a)
    kv_n = kv_tm[n_p:n_tok].reshape(bs, 1, 2, ng, heads_a, hd_a)
    for gi, (win, dil) in enumerate(groups):
        keep = min(win, seq)
        prompt_kv.append(kv_p[:, seq - keep:, :, gi])
        sample_kv.append(kv_n[:, :, :, gi])
    return (y_prompt, y_sample, prompt_wkv, prompt_shift, prompt_kv[0], prompt_kv[1], prompt_kv[2],
            sample_wkv, sample_shift, sample_kv[0], sample_kv[1], sample_kv[2])
```

```python
import functools

import jax
import jax.numpy as jnp
from jax import lax
from jax.experimental import pallas as pl
from jax.experimental.pallas import tpu as pltpu

F32 = jnp.float32
BF16 = jnp.bfloat16

LANES = 128
SUBLANES = 8
VMEM_LIMIT = 56 * 1024 * 1024

LN_EPS = 1e-5
ROPE_THETA = 10000.0
PAST_LEN = 16384
MOE_ROWS = 512


def _cparams(sem):
    return pltpu.CompilerParams(dimension_semantics=sem, vmem_limit_bytes=VMEM_LIMIT)


def _mm_body(x_ref, w_ref, *rest, nk, k_rem, epilogue, has_bias):
    if has_bias:
        b_ref, o_ref, acc_ref = rest
    else:
        o_ref, acc_ref = rest
    k = pl.program_id(2)

    @pl.when(k == 0)
    def _init():
        acc_ref[...] = jnp.zeros_like(acc_ref)

    def accumulate(x, w):
        acc_ref[...] += jnp.dot(x.astype(BF16), w.astype(BF16), preferred_element_type=F32)

    if k_rem == 0:
        accumulate(x_ref[...], w_ref[...])
    else:
        @pl.when(k < nk - 1)
        def _full():
            accumulate(x_ref[...], w_ref[...])

        @pl.when(k == nk - 1)
        def _edge():
            x, w = x_ref[...], w_ref[...]
            col = lax.broadcasted_iota(jnp.int32, x.shape, 1)
            row = lax.broadcasted_iota(jnp.int32, w.shape, 0)
            accumulate(jnp.where(col < k_rem, x, jnp.zeros_like(x)),
                       jnp.where(row < k_rem, w, jnp.zeros_like(w)))

    @pl.when(k == nk - 1)
    def _fin():
        y = acc_ref[...]
        if has_bias:
            y = y + b_ref[...]
        if epilogue is not None:
            y = epilogue(y)
        o_ref[...] = y.astype(o_ref.dtype)


def _matmul(x, w3, li, *, tm, tn, tk, out_dtype=F32, bias=None, epilogue=None, name="mm"):
    m, kdim = x.shape
    _, kw, n = w3.shape
    assert kw == kdim and m % tm == 0 and n % tn == 0
    nk = pl.cdiv(kdim, tk)
    in_specs = [pl.BlockSpec((tm, tk), lambda i, j, k: (i, k)),
                pl.BlockSpec((None, tk, tn), lambda i, j, k: (li, k, j))]
    args = [x, w3]
    if bias is not None:
        in_specs.append(pl.BlockSpec((1, tn), lambda i, j, k: (0, j)))
        args.append(bias.reshape(1, n).astype(F32))
    return pl.pallas_call(
        functools.partial(_mm_body, nk=nk, k_rem=kdim % tk, epilogue=epilogue, has_bias=bias is not None),
        grid=(m // tm, n // tn, nk),
        in_specs=in_specs,
        out_specs=pl.BlockSpec((tm, tn), lambda i, j, k: (i, j)),
        out_shape=jax.ShapeDtypeStruct((m, n), out_dtype),
        scratch_shapes=[pltpu.VMEM((tm, tn), F32)],
        compiler_params=_cparams(("parallel", "parallel", "arbitrary")),
        name=name,
    )(*args)


def _mm_heads_body(x_ref, w_ref, cos_ref, sin_ref, o_ref, acc_ref, *, nk, n_rope, hd):
    j = pl.program_id(1)
    k = pl.program_id(2)

    @pl.when(k == 0)
    def _init():
        acc_ref[...] = jnp.zeros_like(acc_ref)

    acc_ref[...] += jnp.dot(x_ref[...].astype(BF16), w_ref[...].astype(BF16),
                            preferred_element_type=F32)
    n_heads = acc_ref.shape[1] // hd

    @pl.when((k == nk - 1) & (j < n_rope))
    def _fin_rope():
        cos, sin = cos_ref[...], sin_ref[...]
        for h in range(n_heads):
            y = acc_ref[:, h * hd:(h + 1) * hd]
            o_ref[h] = y * cos + pltpu.roll(y, hd // 2, 1) * sin

    @pl.when((k == nk - 1) & (j >= n_rope))
    def _fin_plain():
        for h in range(n_heads):
            o_ref[h] = acc_ref[:, h * hd:(h + 1) * hd]


def _matmul_heads(x, w3, li, cos, sin_s, *, n_rope, tm, tn, tk, hd, name):
    m, kdim = x.shape
    _, _, n = w3.shape
    assert m % tm == 0 and n % tn == 0 and kdim % tk == 0 and tn % hd == 0
    nk = kdim // tk
    hpt = tn // hd
    return pl.pallas_call(
        functools.partial(_mm_heads_body, nk=nk, n_rope=n_rope, hd=hd),
        grid=(m // tm, n // tn, nk),
        in_specs=[pl.BlockSpec((tm, tk), lambda i, j, k: (i, k)),
                  pl.BlockSpec((None, tk, tn), lambda i, j, k: (li, k, j)),
                  pl.BlockSpec((tm, hd), lambda i, j, k: (i, 0)),
                  pl.BlockSpec((tm, hd), lambda i, j, k: (i, 0))],
        out_specs=pl.BlockSpec((hpt, tm, hd), lambda i, j, k: (j, i, 0)),
        out_shape=jax.ShapeDtypeStruct((n // hd, m, hd), F32),
        scratch_shapes=[pltpu.VMEM((tm, tn), F32)],
        compiler_params=_cparams(("parallel", "parallel", "arbitrary")),
        name=name,
    )(x, w3, cos, sin_s)


def _window_body(x_ref, o_ref, *, heads, hd):
    for h in range(heads):
        o_ref[:, h * hd:(h + 1) * hd] = x_ref[h]


def _kv_window(kv_hm, *, group, n_groups, heads, batch, seq, keep, name):
    hd = kv_hm.shape[-1]
    tmc = min(keep, 1024)
    assert keep % tmc == 0 and (seq - keep) % tmc == 0
    per_b = keep // tmc

    def in_map(i, c):
        b, t = i // per_b, i % per_b
        return (c * n_groups + group, (b * seq + seq - keep) // tmc + t, 0)

    return pl.pallas_call(
        functools.partial(_window_body, heads=heads, hd=hd),
        grid=(batch * per_b, 2),
        in_specs=[pl.BlockSpec((heads, tmc, hd), in_map)],
        out_specs=pl.BlockSpec((tmc, heads * hd), lambda i, c: (i, c)),
        out_shape=jax.ShapeDtypeStruct((batch * keep, 2 * heads * hd), F32),
        compiler_params=_cparams(("parallel", "parallel")),
        name=name,
    )(kv_hm)


def _swiglu_body(x_ref, wg_ref, wu_ref, o_ref, accg_ref, accu_ref, *, nk):
    k = pl.program_id(2)

    @pl.when(k == 0)
    def _init():
        accg_ref[...] = jnp.zeros_like(accg_ref)
        accu_ref[...] = jnp.zeros_like(accu_ref)

    xb = x_ref[...].astype(BF16)
    accg_ref[...] += jnp.dot(xb, wg_ref[...].astype(BF16), preferred_element_type=F32)
    accu_ref[...] += jnp.dot(xb, wu_ref[...].astype(BF16), preferred_element_type=F32)

    @pl.when(k == nk - 1)
    def _fin():
        g = accg_ref[...]
        o_ref[...] = (g * jax.nn.sigmoid(g) * accu_ref[...]).astype(o_ref.dtype)


def _swiglu_in(x, w3, li, *, tm, tn, tk, name="swiglu_in"):
    m, kdim = x.shape
    _, _, n2 = w3.shape
    hid = n2 // 2
    assert m % tm == 0 and hid % tn == 0 and kdim % tk == 0
    nk = kdim // tk
    nt = hid // tn
    return pl.pallas_call(
        functools.partial(_swiglu_body, nk=nk),
        grid=(m // tm, nt, nk),
        in_specs=[pl.BlockSpec((tm, tk), lambda i, j, k: (i, k)),
                  pl.BlockSpec((None, tk, tn), lambda i, j, k: (li, k, j)),
                  pl.BlockSpec((None, tk, tn), lambda i, j, k: (li, k, j + nt))],
        out_specs=pl.BlockSpec((tm, tn), lambda i, j, k: (i, j)),
        out_shape=jax.ShapeDtypeStruct((m, hid), BF16),
        scratch_shapes=[pltpu.VMEM((tm, tn), F32), pltpu.VMEM((tm, tn), F32)],
        compiler_params=_cparams(("parallel", "parallel", "arbitrary")),
        name=name,
    )(x, w3, w3)


def _ln_body(x_ref, f_ref, g_ref, b_ref, o_ref, ob_ref, *, alpha):
    z = alpha * x_ref[...] + f_ref[...]
    mu = jnp.mean(z, -1, keepdims=True)
    zc = z - mu
    var = jnp.mean(zc * zc, -1, keepdims=True)
    y = zc * lax.rsqrt(var + LN_EPS) * g_ref[...] + b_ref[...]
    o_ref[...] = y
    ob_ref[...] = y.astype(BF16)


def _res_ln(x, f, g, b, *, alpha, tm, name="res_ln"):
    m, d = x.shape
    assert m % tm == 0
    row = pl.BlockSpec((tm, d), lambda i: (i, 0))
    vec = pl.BlockSpec((1, d), lambda i: (0, 0))
    return pl.pallas_call(
        functools.partial(_ln_body, alpha=alpha),
        grid=(m // tm,),
        in_specs=[row, row, vec, vec],
        out_specs=[row, row],
        out_shape=[jax.ShapeDtypeStruct((m, d), F32), jax.ShapeDtypeStruct((m, d), BF16)],
        compiler_params=_cparams(("parallel",)),
        name=name,
    )(x, f, g.reshape(1, d), b.reshape(1, d))


def _combine_ln_body(x_ref, ya_ref, yb_ref, gt_ref, g_ref, b_ref, o_ref, *, alpha):
    gt = gt_ref[...]
    f = ya_ref[...] * gt[:, 0:1] + yb_ref[...] * gt[:, 1:2]
    z = alpha * x_ref[...] + f
    mu = jnp.mean(z, -1, keepdims=True)
    zc = z - mu
    var = jnp.mean(zc * zc, -1, keepdims=True)
    o_ref[...] = zc * lax.rsqrt(var + LN_EPS) * g_ref[...] + b_ref[...]


def _combine_ln(x, ya, yb, gates, g, b, *, alpha, tm, name="moe_combine_ln"):
    m, d = x.shape
    assert m % tm == 0
    row = pl.BlockSpec((tm, d), lambda i: (i, 0))
    vec = pl.BlockSpec((1, d), lambda i: (0, 0))
    return pl.pallas_call(
        functools.partial(_combine_ln_body, alpha=alpha),
        grid=(m // tm,),
        in_specs=[row, row, row, pl.BlockSpec((tm, gates.shape[1]), lambda i: (i, 0)), vec, vec],
        out_specs=row,
        out_shape=jax.ShapeDtypeStruct((m, d), F32),
        compiler_params=_cparams(("parallel",)),
        name=name,
    )(x, ya, yb, gates, g.reshape(1, d), b.reshape(1, d))


def _wkv_body(r_ref, w_ref, k_ref, v_ref, a_ref, b_ref, s0_ref, y_ref, s_ref, *, tc, n):
    t_blk = pl.program_id(1)

    @pl.when(t_blk == 0)
    def _init():
        s_ref[...] = s0_ref[...]

    def row(ref, t, j):
        return ref[t, pl.ds(j, 1), :]

    def step(t, carry):
        sa = jnp.zeros((n, LANES), F32)
        for j in range(n):
            sa = sa + s_ref[j] * row(a_ref, t, j)
        v_t = v_ref[t]
        y = jnp.zeros((n, LANES), F32)
        for j in range(n):
            s_new = s_ref[j] * row(w_ref, t, j) + sa * row(b_ref, t, j) + v_t * row(k_ref, t, j)
            s_ref[j] = s_new
            y = y + s_new * row(r_ref, t, j)
        y_ref[t] = y
        return carry

    lax.fori_loop(0, tc, step, 0)


def _wkv_scan(r, w, k, v, a, b, s0, *, tc, name="wkv_scan"):
    t_len, n, lanes = r.shape
    assert lanes % LANES == 0 and t_len % tc == 0
    seq = pl.BlockSpec((tc, n, LANES), lambda l, t: (t, 0, l))
    st = pl.BlockSpec((n, n, LANES), lambda l, t: (0, 0, l))
    return pl.pallas_call(
        functools.partial(_wkv_body, tc=tc, n=n),
        grid=(lanes // LANES, t_len // tc),
        in_specs=[seq] * 6 + [st],
        out_specs=[seq, st],
        out_shape=[jax.ShapeDtypeStruct((t_len, n, lanes), F32),
                   jax.ShapeDtypeStruct((n, n, lanes), F32)],
        compiler_params=_cparams(("parallel", "arbitrary")),
        name=name,
    )(r, w, k, v, a, b, s0)


CHUNK = 128
SUB = 32


def _head_normalize(x_t, hl, n):
    x3 = x_t.reshape(hl, n, CHUNK)
    nrm = jnp.sqrt(jnp.sum(x3 * x3, axis=1, keepdims=True))
    return (x3 / jnp.maximum(nrm, 1e-12)).reshape(hl * n, CHUNK)


def _to_scan_body(*refs, mode, nb, hl, n):
    n_in = {"copy": 1, "k2": 2, "kk": 1, "kb": 2}[mode]
    tiles = [refs[i * nb:(i + 1) * nb] for i in range(n_in)]
    pos = n_in * nb
    par_ref = None
    if mode != "copy":
        par_ref = refs[pos]
        pos += 1
    o_ref, sc_ref = refs[pos], refs[pos + 1]

    for b in range(nb):
        if mode == "copy":
            res = tiles[0][b][...].T
        elif mode == "k2":
            k, a = tiles[0][b][...], tiles[1][b][...]
            res = (k * (1.0 + (a - 1.0) * par_ref[...])).T
        elif mode == "kk":
            res = _head_normalize((tiles[0][b][...] * par_ref[...]).T, hl, n)
        else:
            kk = _head_normalize((tiles[0][b][...] * par_ref[...]).T, hl, n)
            res = kk * tiles[1][b][...].T
        sc_ref[b] = res

    for j in range(n):
        slab = jnp.concatenate([sc_ref.at[b][pl.ds(j, hl, stride=n), :] for b in range(nb)], axis=0)
        slab_t = slab.T
        for s in range(CHUNK // SUB):
            o_ref[s, pl.ds(j * SUB, SUB), :] = slab_t[s * SUB:(s + 1) * SUB, :]


def _to_scan(inputs, par, *, mode, nb, seq, heads, n, name):
    hl = LANES // nb
    width = hl * n
    n_hh = heads // hl
    n_c = seq // CHUNK
    in_specs, args = [], []
    for x in inputs:
        for b in range(nb):
            in_specs.append(pl.BlockSpec((CHUNK, width), lambda hh, c, b=b: (b * n_c + c, hh)))
            args.append(x)
    if par is not None:
        in_specs.append(pl.BlockSpec((1, width), lambda hh, c: (0, hh)))
        args.append(par.reshape(1, heads * n))
    return pl.pallas_call(
        functools.partial(_to_scan_body, mode=mode, nb=nb, hl=hl, n=n),
        grid=(n_hh, n_c),
        in_specs=in_specs,
        out_specs=pl.BlockSpec((CHUNK // SUB, n * SUB, LANES), lambda hh, c: (c, 0, hh)),
        out_shape=jax.ShapeDtypeStruct((seq // SUB, n * SUB, n_hh * LANES), F32),
        scratch_shapes=[pltpu.VMEM((nb, width, CHUNK), F32)],
        compiler_params=_cparams(("parallel", "parallel")),
        name=name,
    )(*args)


def _wkv_sl_body(r_ref, w_ref, k_ref, v_ref, kk_ref, b_ref, y_ref, s_ref, *, n):
    @pl.when(pl.program_id(1) == 0)
    def _init():
        s_ref[...] = jnp.zeros_like(s_ref)

    def row(ref, t, j):
        return ref[pl.ds(j * SUB + t, 1), :]

    def step(t, skk):
        t_next = jnp.minimum(t + 1, SUB - 1)
        v_t = v_ref[pl.ds(t, n, stride=SUB), :]
        y = jnp.zeros((n, LANES), F32)
        skk_next = jnp.zeros((n, LANES), F32)
        for j in range(n):
            s_new = s_ref[j] * row(w_ref, t, j) - skk * row(b_ref, t, j) + v_t * row(k_ref, t, j)
            s_ref[j] = s_new
            y = y + s_new * row(r_ref, t, j)
            skk_next = skk_next + s_new * row(kk_ref, t_next, j)
        y_ref[pl.ds(t, n, stride=SUB), :] = y
        return skk_next

    skk0 = jnp.zeros((n, LANES), F32)
    for j in range(n):
        skk0 = skk0 + s_ref[j] * row(kk_ref, 0, j)
    lax.fori_loop(0, SUB, step, skk0)


def _wkv_scan_sl(r, w, k, v, kk, b, *, n, name="wkv_scan_prompt"):
    n_blk, rows, lanes = r.shape
    seq = pl.BlockSpec((None, rows, LANES), lambda l, t: (t, 0, l))
    st = pl.BlockSpec((n, n, LANES), lambda l, t: (0, 0, l))
    return pl.pallas_call(
        functools.partial(_wkv_sl_body, n=n),
        grid=(lanes // LANES, n_blk),
        in_specs=[seq] * 6,
        out_specs=[seq, st],
        out_shape=[jax.ShapeDtypeStruct((n_blk, rows, lanes), F32),
                   jax.ShapeDtypeStruct((n, n, lanes), F32)],
        compiler_params=_cparams(("parallel", "arbitrary")),
        name=name,
    )(r, w, k, v, kk, b)


def _from_scan_body(y_ref, r_ref, k_ref, v_ref, rk_ref, lg_ref, lb_ref, g_ref, o_ref, z_ref, sc_ref,
                    *, nb, hl, n, eps):
    b = pl.program_id(2)

    @pl.when(b == 0)
    def _fill():
        for s in range(CHUNK // SUB):
            def blk(ref, j, s=s):
                return ref[s, pl.ds(j * SUB, SUB), :]

            coef = jnp.zeros((SUB, LANES), F32)
            tot = jnp.zeros((SUB, LANES), F32)
            for j in range(n):
                coef = coef + blk(r_ref, j) * blk(k_ref, j) * rk_ref[pl.ds(j, 1), :]
                tot = tot + blk(y_ref, j)
            mu = tot / n
            var = jnp.zeros((SUB, LANES), F32)
            for j in range(n):
                d = blk(y_ref, j) - mu
                var = var + d * d
            inv = lax.rsqrt(var / n + eps)
            for j in range(n):
                z = (blk(y_ref, j) - mu) * inv * lg_ref[pl.ds(j, 1), :] + lb_ref[pl.ds(j, 1), :]
                z_ref[j, pl.ds(s * SUB, SUB), :] = z + coef * blk(v_ref, j)
        for j in range(n):
            slab_t = z_ref[j].T
            for bb in range(nb):
                sc_ref.at[bb][pl.ds(j, hl, stride=n), :] = slab_t[bb * hl:(bb + 1) * hl, :]

    o_ref[...] = (sc_ref[b].T * g_ref[...]).astype(o_ref.dtype)


def _from_scan(y, r, k2, v, rk_t, lg_t, lb_t, g, *, nb, seq, heads, n, eps, name="rwkv_post"):
    hl = LANES // nb
    width = hl * n
    n_hh = heads // hl
    n_c = seq // CHUNK
    sl = pl.BlockSpec((CHUNK // SUB, n * SUB, LANES), lambda hh, c, b: (c, 0, hh))
    par = pl.BlockSpec((n, LANES), lambda hh, c, b: (0, hh))
    tok = pl.BlockSpec((CHUNK, width), lambda hh, c, b: (b * n_c + c, hh))
    return pl.pallas_call(
        functools.partial(_from_scan_body, nb=nb, hl=hl, n=n, eps=eps),
        grid=(n_hh, n_c, nb),
        in_specs=[sl, sl, sl, sl, par, par, par, tok],
        out_specs=tok,
        out_shape=jax.ShapeDtypeStruct((nb * seq, heads * n), BF16),
        scratch_shapes=[pltpu.VMEM((n, CHUNK, LANES), F32), pltpu.VMEM((nb, width, CHUNK), F32)],
        compiler_params=_cparams(("parallel", "parallel", "arbitrary")),
        name=name,
    )(y, r, k2, v, rk_t, lg_t, lb_t, g)


def _head_param_lanes(p, nb, heads, n):
    hl = LANES // nb
    t = p.reshape(heads // hl, 1, hl, n)
    t = jnp.broadcast_to(t, (heads // hl, nb, hl, n))
    return jnp.transpose(t, (3, 0, 1, 2)).reshape(n, (heads // hl) * LANES).astype(F32)


SPAN = 128
ATTN_UNROLL = 3


def _attn_block(q, k, v, causal_own, scale):
    s = lax.dot_general(q.astype(BF16), k.astype(BF16), (((1,), (1,)), ((), ())),
                        preferred_element_type=F32) * scale
    qi = lax.broadcasted_iota(jnp.int32, s.shape, 0)
    ki = lax.broadcasted_iota(jnp.int32, s.shape, 1)
    if causal_own:
        mask = ki <= qi
    else:
        mask = (ki >= qi) & (ki <= qi + SPAN)
    s = jnp.where(mask, s, -jnp.inf)
    m = jnp.max(s, axis=1, keepdims=True)
    p = jnp.exp(s - m)
    l = jnp.sum(p, axis=1, keepdims=True)
    o = jnp.dot(p.astype(BF16), v.astype(BF16), preferred_element_type=F32)
    return o, m, l


def _dilated_attn_body(*refs, dils, seq, scale):
    ng = len(dils)
    q_refs, k_refs, v_refs = refs[:ng], refs[ng:2 * ng], refs[2 * ng:3 * ng]
    o_ref = refs[3 * ng]
    scratch = refs[3 * ng + 1:]
    acc_refs, m_refs, l_refs = scratch[:ng], scratch[ng:2 * ng], scratch[2 * ng:]

    for g, dil in enumerate(dils):
        q_ref, k_ref, v_ref = q_refs[g], k_refs[g], v_refs[g]
        acc_ref, m_ref, l_ref = acc_refs[g], m_refs[g], l_refs[g]
        nblk = seq // (dil * SPAN)

        def rows(start, size, dil=dil):
            if dil == 1:
                return pl.ds(start, size)
            return pl.ds(start, size, stride=dil)

        def put(sl, res, acc_ref=acc_ref, m_ref=m_ref, l_ref=l_ref):
            o, m, l = res
            acc_ref[sl, :] = o
            m_ref[sl, :] = m
            l_ref[sl, :] = l

        for r in range(dil):
            sl0 = rows(r, SPAN)
            put(sl0, _attn_block(q_ref[sl0, :], k_ref[sl0, :], v_ref[sl0, :], True, scale))

            def body(nb, carry, r=r, dil=dil, rows=rows, put=put, q_ref=q_ref, k_ref=k_ref, v_ref=v_ref):
                q_sl = rows(r + dil * SPAN * nb, SPAN)
                kv_sl = rows(r + dil * SPAN * (nb - 1), 2 * SPAN)
                put(q_sl, _attn_block(q_ref[q_sl, :], k_ref[kv_sl, :], v_ref[kv_sl, :], False, scale))
                return carry

            if nblk > 1:
                lax.fori_loop(1, nblk, body, 0, unroll=ATTN_UNROLL)

    m_all = m_refs[0][...]
    for g in range(1, ng):
        m_all = jnp.maximum(m_all, m_refs[g][...])
    num = jnp.zeros(acc_refs[0].shape, F32)
    den = jnp.zeros(m_all.shape, F32)
    for g in range(ng):
        e = jnp.exp(m_refs[g][...] - m_all)
        num = num + e * acc_refs[g][...]
        den = den + e * l_refs[g][...]
    o_ref[...] = (num / den).astype(o_ref.dtype)


def _dilated_attn(q_hm, k_hm, v_hm, *, batch, seq, heads, dils, v_head0=0, name="dilated_attn"):
    e = q_hm.shape[-1]
    ng = len(dils)

    def spec(g, h0=0):
        return pl.BlockSpec((None, seq, e), lambda b, h, g=g: (h0 + g * heads + h, b, 0))

    specs = [spec(g) for g in range(ng)]
    return pl.pallas_call(
        functools.partial(_dilated_attn_body, dils=dils, seq=seq, scale=e ** -0.5),
        grid=(batch, heads),
        in_specs=specs * 2 + [spec(g, v_head0) for g in range(ng)],
        out_specs=pl.BlockSpec((seq, e), lambda b, h: (b, h)),
        out_shape=jax.ShapeDtypeStruct((batch * seq, heads * e), BF16),
        scratch_shapes=([pltpu.VMEM((seq, e), F32)] * ng + [pltpu.VMEM((seq, 1), F32)] * (2 * ng)),
        compiler_params=_cparams(("parallel", "parallel")),
        name=name,
    )(*([q_hm] * ng + [k_hm] * ng + [v_hm] * ng))


def _sample_attn_body(q_ref, kn_ref, vn_ref, *rest, ng, heads, scale):
    cache_refs, o_ref = rest[:ng], rest[ng]
    for h in range(heads):
        parts = []
        for g in range(ng):
            row = g * heads + h
            q = q_ref[pl.ds(row, 1), :]
            kc = cache_refs[g][:, 0, h, :]
            vc = cache_refs[g][:, 1, h, :]
            kn, vn = kn_ref[pl.ds(row, 1), :], vn_ref[pl.ds(row, 1), :]
            qb = q.astype(BF16)
            s = lax.dot_general(qb, kc.astype(BF16), (((1,), (1,)), ((), ())),
                                preferred_element_type=F32) * scale
            s_n = jnp.sum(qb.astype(F32) * kn.astype(BF16).astype(F32), axis=1, keepdims=True) * scale
            m = jnp.maximum(jnp.max(s, axis=1, keepdims=True), s_n)
            p, p_n = jnp.exp(s - m), jnp.exp(s_n - m)
            l = jnp.sum(p, axis=1, keepdims=True) + p_n
            o = jnp.dot(p.astype(BF16), vc.astype(BF16), preferred_element_type=F32)
            o = o + p_n.astype(BF16).astype(F32) * vn.astype(BF16).astype(F32)
            parts.append((o, m, l))
        m_all = parts[0][1]
        for _, m, _ in parts[1:]:
            m_all = jnp.maximum(m_all, m)
        num = jnp.zeros_like(parts[0][0])
        den = jnp.zeros_like(m_all)
        for o, m, l in parts:
            e = jnp.exp(m - m_all)
            num = num + e * o
            den = den + e * l
        o_ref[pl.ds(h, 1), :] = (num / den).astype(o_ref.dtype)


def _sample_attn(q_s, k_new, v_new, caches, dils, *, heads, name="sample_attn"):
    bs, _, e = q_s.shape
    ng = len(dils)
    tok = pl.BlockSpec((None, ng * heads, e), lambda b: (b, 0, 0))
    in_specs, args = [tok, tok, tok], [q_s, k_new, v_new]
    for c, dil in zip(caches, dils):
        w = c.shape[1]
        assert w == SPAN * dil
        args.append(c.reshape(bs, SPAN, dil, 2, heads, e))
        in_specs.append(pl.BlockSpec((None, SPAN, None, 2, heads, e), lambda b: (b, 0, 0, 0, 0, 0)))
    return pl.pallas_call(
        functools.partial(_sample_attn_body, ng=ng, heads=heads, scale=e ** -0.5),
        grid=(bs,),
        in_specs=in_specs,
        out_specs=pl.BlockSpec((None, heads, e), lambda b: (b, 0, 0)),
        out_shape=jax.ShapeDtypeStruct((bs, heads, e), BF16),
        compiler_params=_cparams(("parallel",)),
        name=name,
    )(*args)


def _router_body(x_ref, w_ref, o_ref):
    o_ref[...] = jnp.dot(x_ref[...], w_ref[...], precision=lax.Precision.HIGHEST,
                         preferred_element_type=F32)


def _router_logits(x, router, *, tm, name="router_logits"):
    m, d = x.shape
    n_exp = router.shape[1]
    w = jnp.zeros((d, LANES), F32).at[:, :n_exp].set(router.astype(F32))
    out = pl.pallas_call(
        _router_body,
        grid=(m // tm,),
        in_specs=[pl.BlockSpec((tm, d), lambda i: (i, 0)), pl.BlockSpec((d, LANES), lambda i: (0, 0))],
        out_specs=pl.BlockSpec((tm, LANES), lambda i: (i, 0)),
        out_shape=jax.ShapeDtypeStruct((m, LANES), F32),
        compiler_params=_cparams(("parallel",)),
        name=name,
    )(x, w)
    return out[:, :n_exp]


def _moe_in_body(be_ref, ok_ref, src_ref, x_ref, wg_ref, wu_ref, o_ref, wgb_ref, wub_ref):
    r = pl.program_id(1)
    changed = (r == 0) | (be_ref[r] != be_ref[jnp.maximum(r - 1, 0)])

    @pl.when(changed)
    def _cast():
        wgb_ref[...] = wg_ref[...].astype(BF16)
        wub_ref[...] = wu_ref[...].astype(BF16)

    @pl.when(ok_ref[r] != 0)
    def _compute():
        x = x_ref[...]
        g = jnp.dot(x, wgb_ref[...], preferred_element_type=F32)
        u = jnp.dot(x, wub_ref[...], preferred_element_type=F32)
        o_ref[...] = (g * jax.nn.sigmoid(g) * u).astype(o_ref.dtype)

    @pl.when(ok_ref[r] == 0)
    def _skip():
        o_ref[...] = jnp.zeros_like(o_ref)


def _moe_in(x_buf, w_in, blk_e, blk_ok, blk_src, *, tn, name="moe_swiglu_in"):
    rows, d = x_buf.shape
    n_exp, _, n2 = w_in.shape
    hid = n2 // 2
    nt = hid // tn
    rb = rows // MOE_ROWS
    grid_spec = pltpu.PrefetchScalarGridSpec(
        num_scalar_prefetch=3,
        grid=(nt, rb),
        in_specs=[pl.BlockSpec((MOE_ROWS, d), lambda j, r, be, ok, src: (src[r], 0)),
                  pl.BlockSpec((None, d, tn), lambda j, r, be, ok, src: (be[r], 0, j)),
                  pl.BlockSpec((None, d, tn), lambda j, r, be, ok, src: (be[r], 0, j + nt))],
        out_specs=pl.BlockSpec((MOE_ROWS, tn), lambda j, r, be, ok, src: (r, j)),
        scratch_shapes=[pltpu.VMEM((d, tn), BF16), pltpu.VMEM((d, tn), BF16)],
    )
    return pl.pallas_call(
        _moe_in_body,
        grid_spec=grid_spec,
        out_shape=jax.ShapeDtypeStruct((rows, hid), BF16),
        compiler_params=_cparams(("arbitrary", "arbitrary")),
        name=name,
    )(blk_e, blk_ok, blk_src, x_buf, w_in, w_in)


def _moe_out_body(pe_ref, nv_ref, src_ref, h_ref, w_ref, o_ref, acc_ref, *, nk):
    p = pl.program_id(0)
    k = pl.program_id(2)
    nv = nv_ref[p]

    @pl.when(k == 0)
    def _init():
        acc_ref[...] = jnp.zeros_like(acc_ref)

    @pl.when(nv == 2)
    def _both():
        acc_ref[...] += jnp.dot(h_ref[...], w_ref[...].astype(BF16), preferred_element_type=F32)

    @pl.when(nv == 1)
    def _first():
        acc_ref[:MOE_ROWS, :] += jnp.dot(h_ref[:MOE_ROWS, :], w_ref[...].astype(BF16),
                                         preferred_element_type=F32)

    @pl.when(k == nk - 1)
    def _fin():
        o_ref[...] = acc_ref[...]


def _moe_out(h_buf, w_out, pair_e, pair_nv, pair_src, *, tn, tk, name="moe_out"):
    rows, hid = h_buf.shape
    n_exp, _, d = w_out.shape
    n_pair = rows // (2 * MOE_ROWS)
    nk = hid // tk

    def k_eff(p, k, nv):
        return jnp.where(nv[p] > 0, k, nk - 1)

    grid_spec = pltpu.PrefetchScalarGridSpec(
        num_scalar_prefetch=3,
        grid=(n_pair, d // tn, nk),
        in_specs=[pl.BlockSpec((2 * MOE_ROWS, tk), lambda p, j, k, pe, nv, src: (src[p], k_eff(p, k, nv))),
                  pl.BlockSpec((None, tk, tn), lambda p, j, k, pe, nv, src: (pe[p], k_eff(p, k, nv), j))],
        out_specs=pl.BlockSpec((2 * MOE_ROWS, tn), lambda p, j, k, pe, nv, src: (p, j)),
        scratch_shapes=[pltpu.VMEM((2 * MOE_ROWS, tn), F32)],
    )
    return pl.pallas_call(
        functools.partial(_moe_out_body, nk=nk),
        grid_spec=grid_spec,
        out_shape=jax.ShapeDtypeStruct((rows, d), F32),
        compiler_params=_cparams(("arbitrary", "arbitrary", "arbitrary")),
        name=name,
    )(pair_e, pair_nv, pair_src, h_buf, w_out)


def _round_bf16_body(x_ref, o_ref):
    o_ref[...] = x_ref[...].astype(BF16)


def _round_bf16(x, *, tm, name="round_bf16"):
    m, d = x.shape
    assert m % tm == 0
    spec = pl.BlockSpec((tm, d), lambda i: (i, 0))
    return pl.pallas_call(_round_bf16_body, grid=(m // tm,), in_specs=[spec], out_specs=spec,
                          out_shape=jax.ShapeDtypeStruct((m, d), BF16),
                          compiler_params=_cparams(("parallel",)), name=name)(x)


def _moe(x_f32, n_tok, router, w_in, w_out):
    rows, d = x_f32.shape
    n_exp = router.shape[1]
    top_k = 2
    logits = _router_logits(x_f32, router, tm=520)[:n_tok]
    top_v, top_i = lax.top_k(logits, top_k)
    gates = jax.nn.softmax(top_v, axis=-1)
    n_asg = n_tok * top_k
    e = top_i.reshape(n_asg).astype(jnp.int32)
    tok = jnp.repeat(jnp.arange(n_tok, dtype=jnp.int32), top_k)
    order = jnp.argsort(e)
    e_s, tok_s = e[order], tok[order]
    counts = jnp.bincount(e, length=n_exp).astype(jnp.int32)
    starts = jnp.cumsum(counts) - counts
    padded = -(-counts // MOE_ROWS) * MOE_ROWS
    region = -(-padded // (2 * MOE_ROWS)) * (2 * MOE_ROWS)
    region_ends = jnp.cumsum(region)
    region_starts = region_ends - region
    dest = (region_starts[e_s] + jnp.arange(n_asg, dtype=jnp.int32) - starts[e_s]).astype(jnp.int32)
    n_pair = (-(-n_asg // MOE_ROWS) + 2 * n_exp + 1) // 2
    n_blk = 2 * n_pair
    src_tok = jnp.zeros((n_blk * MOE_ROWS,), jnp.int32).at[dest].set(tok_s)
    pos = jnp.zeros((n_asg,), jnp.int32).at[order].set(dest).reshape(n_tok, top_k)
    blk_start = jnp.arange(n_blk, dtype=jnp.int32) * MOE_ROWS
    blk_e = jnp.minimum(jnp.searchsorted(region_ends, blk_start, side='right'), n_exp - 1).astype(jnp.int32)
    blk_ok = ((blk_start - region_starts[blk_e]) < padded[blk_e]).astype(jnp.int32)
    blk_src = lax.cummax(jnp.where(blk_ok > 0, jnp.arange(n_blk, dtype=jnp.int32), 0))
    pair_nv = blk_ok.reshape(n_pair, 2).sum(axis=1).astype(jnp.int32)
    pair_src = lax.cummax(jnp.where(pair_nv > 0, jnp.arange(n_pair, dtype=jnp.int32), 0))

    x_buf = _round_bf16(x_f32[src_tok], tm=MOE_ROWS)
    h_buf = _moe_in(x_buf, w_in, blk_e, blk_ok, blk_src, tn=512)
    y_buf = _moe_out(h_buf, w_out, blk_e[::2], pair_nv, pair_src, tn=min(2048, d), tk=1024)
    pos = jnp.zeros((rows, top_k), jnp.int32).at[:n_tok].set(pos)
    gates = jnp.zeros((rows, top_k), F32).at[:n_tok].set(gates)
    return y_buf[pos[:, 0]], y_buf[pos[:, 1]], gates


def _rope_tables(pos, hd):
    inv = ROPE_THETA ** (-jnp.arange(0, hd, 2, dtype=F32) / hd)
    ang = pos.astype(F32)[:, None] * inv[None, :]
    ang = jnp.concatenate([ang, ang], -1)
    return jnp.cos(ang), jnp.sin(ang)


def _softplus(z):
    return jnp.maximum(z, 0.0) + jnp.log(1.0 + jnp.exp(-jnp.abs(z)))


def _decay_epilogue(y):
    return jnp.exp(-jnp.exp(-_softplus(-y) - 0.5))


def kernel(x_prompt, x_sample, state_wkv, state_shift, cache_kv_w128, cache_kv_w512, cache_kv_w2048,
           ln_g, ln_b, rwkv_mix, rwkv_w_rkv, rwkv_w0, rwkv_w1, rwkv_w2, rwkv_a0, rwkv_a1, rwkv_a2,
           rwkv_g1, rwkv_g2, rwkv_k_k, rwkv_k_a, rwkv_r_k, rwkv_lnx_g, rwkv_lnx_b, rwkv_w_out,
           attn_w_kv, attn_w_q, attn_w_out, ffn_w_in, ffn_w_out, moe_router, moe_w_in, moe_w_out):
    bp, seq, d = x_prompt.shape
    bs, seq_s, _ = x_sample.shape
    depth = ln_g.shape[0]
    assert depth == 2 and seq_s == 1
    heads_r, hd_r = rwkv_r_k.shape[1], rwkv_r_k.shape[2]
    past_len = PAST_LEN
    groups = ((128, 1), (512, 4), (2048, 16))
    ng = len(groups)
    hd_a = 128
    heads_a = attn_w_out.shape[1] // hd_a
    alpha = (2.0 * depth) ** 0.25
    lnx_eps = 1e-5 * hd_r

    n_p = bp * seq
    n_tok = n_p + bs
    tm = 2080
    rows = -(-n_tok // tm) * tm

    def pad_rows(a):
        return jnp.concatenate([a, jnp.zeros((rows - a.shape[0],) + a.shape[1:], a.dtype)], 0)

    x0 = pad_rows(jnp.concatenate([x_prompt.reshape(n_p, d), x_sample.reshape(bs, d)], 0))
    prev_p = jnp.concatenate([jnp.zeros((bp, 1, d), F32), x_prompt[:, :-1]], 1).reshape(n_p, d)
    x_prev = pad_rows(jnp.concatenate([prev_p, state_shift[0]], 0))
    xx = x_prev - x0
    xs = [(x0 + xx * rwkv_mix[0, c]).astype(BF16) for c in range(6)]

    w_rkv = rwkv_w_rkv.reshape(3, d, d)
    mm = functools.partial(_matmul, tm=tm)
    r = mm(xs[0], w_rkv, 0, tn=1024, tk=1024, name="rwkv_r")
    k = mm(xs[1], w_rkv, 1, tn=1024, tk=1024, name="rwkv_k")
    v = mm(xs[2], w_rkv, 2, tn=1024, tk=1024, name="rwkv_v")
    lw, la, lg = rwkv_w1.shape[2], rwkv_a1.shape[2], rwkv_g1.shape[2]
    hw = mm(xs[3], rwkv_w1, 0, tn=lw, tk=1024, out_dtype=BF16, epilogue=jnp.tanh, name="rwkv_w1")
    decay = mm(hw, rwkv_w2, 0, tn=1024, tk=lw, bias=rwkv_w0[0], epilogue=_decay_epilogue, name="rwkv_w2")
    ha = mm(xs[4], rwkv_a1, 0, tn=la, tk=1024, out_dtype=BF16, name="rwkv_a1")
    a = mm(ha, rwkv_a2, 0, tn=1024, tk=la, bias=rwkv_a0[0], epilogue=jax.nn.sigmoid, name="rwkv_a2")
    hg = mm(xs[5], rwkv_g1, 0, tn=lg, tk=1024, out_dtype=BF16, epilogue=jax.nn.sigmoid, name="rwkv_g1")
    g = mm(hg, rwkv_g2, 0, tn=1024, tk=lg, name="rwkv_g2")

    hl = LANES // bp
    assert LANES % bp == 0 and heads_r % hl == 0 and seq % CHUNK == 0
    kw = dict(nb=bp, seq=seq, heads=heads_r, n=hd_r)
    r_sl = _to_scan([r], None, mode="copy", name="sl_r", **kw)
    w_sl = _to_scan([decay], None, mode="copy", name="sl_w", **kw)
    v_sl = _to_scan([v], None, mode="copy", name="sl_v", **kw)
    k_sl = _to_scan([k, a], rwkv_k_a[0], mode="k2", name="sl_k", **kw)
    kk_sl = _to_scan([k], rwkv_k_k[0], mode="kk", name="sl_kk", **kw)
    b_sl = _to_scan([k, a], rwkv_k_k[0], mode="kb", name="sl_b", **kw)
    y_sl, s_p = _wkv_scan_sl(r_sl, w_sl, k_sl, v_sl, kk_sl, b_sl, n=hd_r)
    tab = functools.partial(_head_param_lanes, nb=bp, heads=heads_r, n=hd_r)
    y_p = _from_scan(y_sl, r_sl, k_sl, v_sl, tab(rwkv_r_k.reshape(-1)), tab(rwkv_lnx_g[0]),
                     tab(rwkv_lnx_b[0]), g, eps=lnx_eps, **kw)
    s_p = s_p.reshape(hd_r, hd_r, heads_r // hl, bp, hl)
    prompt_wkv = jnp.transpose(s_p, (3, 2, 4, 1, 0)).reshape(1, bp, heads_r, hd_r, hd_r).astype(state_wkv.dtype)

    def hs(t):
        return t[n_p:n_tok].reshape(bs, heads_r, hd_r)

    kk_s = hs(k) * rwkv_k_k[0].reshape(heads_r, hd_r)
    kk_s = kk_s / jnp.maximum(jnp.linalg.norm(kk_s, axis=-1, keepdims=True), 1e-12)
    k2_s = hs(k) * (1.0 + (hs(a) - 1.0) * rwkv_k_a[0].reshape(heads_r, hd_r))
    bonus_s = jnp.sum(hs(r) * k2_s * rwkv_r_k[0], -1, keepdims=True) * hs(v)

    def lanes_s(t):
        return jnp.transpose(t, (2, 0, 1)).reshape(1, hd_r, bs * heads_r)

    scan_s = (hs(r), hs(decay), k2_s, hs(v), -kk_s, kk_s * hs(a))
    s0_s = jnp.transpose(state_wkv[0].astype(F32), (3, 2, 0, 1)).reshape(hd_r, hd_r, bs * heads_r)
    y_s, s_s = _wkv_scan(*[lanes_s(t) for t in scan_s], s0_s, tc=1, name="wkv_scan_sample")
    y_s = jnp.transpose(y_s.reshape(hd_r, bs, heads_r), (1, 2, 0))
    sample_wkv = jnp.transpose(s_s.reshape(hd_r, hd_r, bs, heads_r), (2, 3, 1, 0))[None].astype(state_wkv.dtype)
    mu = jnp.mean(y_s, -1, keepdims=True)
    var = jnp.mean(jnp.square(y_s - mu), -1, keepdims=True)
    y_s = ((y_s - mu) * lax.rsqrt(var + lnx_eps)).reshape(bs, d) * rwkv_lnx_g[0] + rwkv_lnx_b[0]
    y_s = ((y_s + bonus_s.reshape(bs, d)) * g[n_p:n_tok]).astype(BF16)

    y = pad_rows(jnp.concatenate([y_p, y_s], 0))
    mix0 = mm(y, rwkv_w_out, 0, tn=1024, tk=1024, name="rwkv_out")
    x1, x1b = _res_ln(x0, mix0, ln_g[0, 0], ln_b[0, 0], alpha=alpha, tm=208, name="ln_0a")

    h = _swiglu_in(x1b, ffn_w_in, 0, tm=tm, tn=256, tk=2048, name="ffn_in")
    f0 = mm(h, ffn_w_out, 0, tn=1024, tk=1024, name="ffn_out")
    x2, x2b = _res_ln(x1, f0, ln_g[0, 1], ln_b[0, 1], alpha=alpha, tm=208, name="ln_0b")

    pos = jnp.concatenate([jnp.tile(jnp.arange(seq, dtype=jnp.int32), bp),
                           jnp.full((bs,), past_len, jnp.int32),
                           jnp.zeros((rows - n_tok,), jnp.int32)])
    cos, sin = _rope_tables(pos, hd_a)
    half_sign = jnp.concatenate([-jnp.ones((hd_a // 2,), F32), jnp.ones((hd_a // 2,), F32)])
    sin_s = sin * half_sign
    n_qh = ng * heads_a
    tn_h = 1024
    kv_hm = _matmul_heads(x2b, attn_w_kv.reshape(1, d, -1), 0, cos, sin_s, n_rope=n_qh * hd_a // tn_h,
                          tm=tm, tn=tn_h, tk=1024, hd=hd_a, name="attn_kv")
    q_hm = _matmul_heads(x2b, attn_w_q, 0, cos, sin_s, n_rope=n_qh * hd_a // tn_h,
                         tm=tm, tn=tn_h, tk=1024, hd=hd_a, name="attn_q")
    dils = tuple(dl for _, dl in groups)
    att_p = _dilated_attn(q_hm, kv_hm, kv_hm, batch=bp, seq=seq, heads=heads_a, dils=dils, v_head0=n_qh)

    caches = (cache_kv_w128, cache_kv_w512, cache_kv_w2048)
    q_s = jnp.transpose(q_hm[:, n_p:n_tok], (1, 0, 2))
    kv_s = jnp.transpose(kv_hm[:, n_p:n_tok], (1, 0, 2)).reshape(bs, 2, n_qh, hd_a)
    att_s = _sample_attn(q_s, kv_s[:, 0], kv_s[:, 1], caches, dils, heads=heads_a)
    att = pad_rows(jnp.concatenate([att_p, att_s.reshape(bs, heads_a * hd_a)], 0))
    mix1 = mm(att, attn_w_out, 0, tn=1024, tk=1024, name="attn_out")
    x3, x3b = _res_ln(x2, mix1, ln_g[1, 0], ln_b[1, 0], alpha=alpha, tm=208, name="ln_1a")

    n_exp = moe_router.shape[-1]
    ya, yb, gates = _moe(x3, n_tok, moe_router.reshape(d, n_exp), moe_w_in.reshape(n_exp, d, -1),
                         moe_w_out.reshape(n_exp, -1, d))
    x4 = _combine_ln(x3, ya, yb, gates, ln_g[1, 1], ln_b[1, 1], alpha=alpha, tm=208)

    y_prompt = x4[:n_p].reshape(bp, seq, d)
    y_sample = x4[n_p:n_tok].reshape(bs, 1, d)
    prompt_shift = x_prompt[:, -1][None]
    sample_shift = x_sample[:, -1][None]
    prompt_kv, sample_kv = [], []
    kv_n = kv_s.reshape(bs, 1, 2, ng, heads_a, hd_a)
    for gi, (win, dil) in enumerate(groups):
        keep = min(win, seq)
        rows_g = _kv_window(kv_hm, group=gi, n_groups=ng, heads=heads_a, batch=bp, seq=seq, keep=keep,
                            name="kv_window_%d" % win)
        prompt_kv.append(rows_g.reshape(bp, keep, 2, heads_a, hd_a))
        sample_kv.append(kv_n[:, :, :, gi])
    return (y_prompt, y_sample, prompt_wkv, prompt_shift, prompt_kv[0], prompt_kv[1], prompt_kv[2],
            sample_wkv, sample_shift, sample_kv[0], sample_kv[1], sample_kv[2])
```

```python
import functools

import jax
import jax.numpy as jnp
from jax import lax
from jax.experimental import pallas as pl
from jax.experimental.pallas import tpu as pltpu

F32 = jnp.float32
BF16 = jnp.bfloat16

LANES = 128
SUBLANES = 8
VMEM_LIMIT = 56 * 1024 * 1024

LN_EPS = 1e-5
ROPE_THETA = 10000.0
PAST_LEN = 16384
MOE_ROWS = 512


def _cparams(sem):
    return pltpu.CompilerParams(dimension_semantics=sem, vmem_limit_bytes=VMEM_LIMIT)


def _mm_body(x_ref, w_ref, *rest, nk, k_rem, epilogue, has_bias):
    if has_bias:
        b_ref, o_ref, acc_ref = rest
    else:
        o_ref, acc_ref = rest
    k = pl.program_id(2)

    @pl.when(k == 0)
    def _init():
        acc_ref[...] = jnp.zeros_like(acc_ref)

    def accumulate(x, w):
        acc_ref[...] += jnp.dot(x.astype(BF16), w.astype(BF16), preferred_element_type=F32)

    if k_rem == 0:
        accumulate(x_ref[...], w_ref[...])
    else:
        @pl.when(k < nk - 1)
        def _full():
            accumulate(x_ref[...], w_ref[...])

        @pl.when(k == nk - 1)
        def _edge():
            x, w = x_ref[...], w_ref[...]
            col = lax.broadcasted_iota(jnp.int32, x.shape, 1)
            row = lax.broadcasted_iota(jnp.int32, w.shape, 0)
            accumulate(jnp.where(col < k_rem, x, jnp.zeros_like(x)),
                       jnp.where(row < k_rem, w, jnp.zeros_like(w)))

    @pl.when(k == nk - 1)
    def _fin():
        y = acc_ref[...]
        if has_bias:
            y = y + b_ref[...]
        if epilogue is not None:
            y = epilogue(y)
        o_ref[...] = y.astype(o_ref.dtype)


def _matmul(x, w3, li, *, tm, tn, tk, out_dtype=F32, bias=None, epilogue=None, name="mm"):
    m, kdim = x.shape
    _, kw, n = w3.shape
    assert kw == kdim and m % tm == 0 and n % tn == 0
    nk = pl.cdiv(kdim, tk)
    in_specs = [pl.BlockSpec((tm, tk), lambda i, j, k: (i, k)),
                pl.BlockSpec((None, tk, tn), lambda i, j, k: (li, k, j))]
    args = [x, w3]
    if bias is not None:
        in_specs.append(pl.BlockSpec((1, tn), lambda i, j, k: (0, j)))
        args.append(bias.reshape(1, n).astype(F32))
    return pl.pallas_call(
        functools.partial(_mm_body, nk=nk, k_rem=kdim % tk, epilogue=epilogue, has_bias=bias is not None),
        grid=(m // tm, n // tn, nk),
        in_specs=in_specs,
        out_specs=pl.BlockSpec((tm, tn), lambda i, j, k: (i, j)),
        out_shape=jax.ShapeDtypeStruct((m, n), out_dtype),
        scratch_shapes=[pltpu.VMEM((tm, tn), F32)],
        compiler_params=_cparams(("parallel", "parallel", "arbitrary")),
        name=name,
    )(*args)


def _mm_heads_body(x_ref, w_ref, cos_ref, sin_ref, o_ref, acc_ref, *, nk, n_rope, hd):
    j = pl.program_id(1)
    k = pl.program_id(2)

    @pl.when(k == 0)
    def _init():
        acc_ref[...] = jnp.zeros_like(acc_ref)

    acc_ref[...] += jnp.dot(x_ref[...].astype(BF16), w_ref[...].astype(BF16),
                            preferred_element_type=F32)
    n_heads = acc_ref.shape[1] // hd

    @pl.when((k == nk - 1) & (j < n_rope))
    def _fin_rope():
        cos, sin = cos_ref[...], sin_ref[...]
        for h in range(n_heads):
            y = acc_ref[:, h * hd:(h + 1) * hd]
            o_ref[h] = y * cos + pltpu.roll(y, hd // 2, 1) * sin

    @pl.when((k == nk - 1) & (j >= n_rope))
    def _fin_plain():
        for h in range(n_heads):
            o_ref[h] = acc_ref[:, h * hd:(h + 1) * hd]


def _matmul_heads(x, w3, li, cos, sin_s, *, n_rope, tm, tn, tk, hd, name):
    m, kdim = x.shape
    _, _, n = w3.shape
    assert m % tm == 0 and n % tn == 0 and kdim % tk == 0 and tn % hd == 0
    nk = kdim // tk
    hpt = tn // hd
    return pl.pallas_call(
        functools.partial(_mm_heads_body, nk=nk, n_rope=n_rope, hd=hd),
        grid=(m // tm, n // tn, nk),
        in_specs=[pl.BlockSpec((tm, tk), lambda i, j, k: (i, k)),
                  pl.BlockSpec((None, tk, tn), lambda i, j, k: (li, k, j)),
                  pl.BlockSpec((tm, hd), lambda i, j, k: (i, 0)),
                  pl.BlockSpec((tm, hd), lambda i, j, k: (i, 0))],
        out_specs=pl.BlockSpec((hpt, tm, hd), lambda i, j, k: (j, i, 0)),
        out_shape=jax.ShapeDtypeStruct((n // hd, m, hd), F32),
        scratch_shapes=[pltpu.VMEM((tm, tn), F32)],
        compiler_params=_cparams(("parallel", "parallel", "arbitrary")),
        name=name,
    )(x, w3, cos, sin_s)


def _window_body(x_ref, o_ref, *, heads, hd):
    for h in range(heads):
        o_ref[:, h * hd:(h + 1) * hd] = x_ref[h]


def _kv_window(kv_hm, *, group, n_groups, heads, batch, seq, keep, name):
    hd = kv_hm.shape[-1]
    tmc = min(keep, 1024)
    assert keep % tmc == 0 and (seq - keep) % tmc == 0
    per_b = keep // tmc

    def in_map(i, c):
        b, t = i // per_b, i % per_b
        return (c * n_groups + group, (b * seq + seq - keep) // tmc + t, 0)

    return pl.pallas_call(
        functools.partial(_window_body, heads=heads, hd=hd),
        grid=(batch * per_b, 2),
        in_specs=[pl.BlockSpec((heads, tmc, hd), in_map)],
        out_specs=pl.BlockSpec((tmc, heads * hd), lambda i, c: (i, c)),
        out_shape=jax.ShapeDtypeStruct((batch * keep, 2 * heads * hd), F32),
        compiler_params=_cparams(("parallel", "parallel")),
        name=name,
    )(kv_hm)


def _swiglu_body(x_ref, wg_ref, wu_ref, o_ref, accg_ref, accu_ref, *, nk):
    k = pl.program_id(2)

    @pl.when(k == 0)
    def _init():
        accg_ref[...] = jnp.zeros_like(accg_ref)
        accu_ref[...] = jnp.zeros_like(accu_ref)

    xb = x_ref[...].astype(BF16)
    accg_ref[...] += jnp.dot(xb, wg_ref[...].astype(BF16), preferred_element_type=F32)
    accu_ref[...] += jnp.dot(xb, wu_ref[...].astype(BF16), preferred_element_type=F32)

    @pl.when(k == nk - 1)
    def _fin():
        g = accg_ref[...]
        o_ref[...] = (g * jax.nn.sigmoid(g) * accu_ref[...]).astype(o_ref.dtype)


def _swiglu_in(x, w3, li, *, tm, tn, tk, name="swiglu_in"):
    m, kdim = x.shape
    _, _, n2 = w3.shape
    hid = n2 // 2
    assert m % tm == 0 and hid % tn == 0 and kdim % tk == 0
    nk = kdim // tk
    nt = hid // tn
    return pl.pallas_call(
        functools.partial(_swiglu_body, nk=nk),
        grid=(m // tm, nt, nk),
        in_specs=[pl.BlockSpec((tm, tk), lambda i, j, k: (i, k)),
                  pl.BlockSpec((None, tk, tn), lambda i, j, k: (li, k, j)),
                  pl.BlockSpec((None, tk, tn), lambda i, j, k: (li, k, j + nt))],
        out_specs=pl.BlockSpec((tm, tn), lambda i, j, k: (i, j)),
        out_shape=jax.ShapeDtypeStruct((m, hid), BF16),
        scratch_shapes=[pltpu.VMEM((tm, tn), F32), pltpu.VMEM((tm, tn), F32)],
        compiler_params=_cparams(("parallel", "parallel", "arbitrary")),
        name=name,
    )(x, w3, w3)


def _ln_body(x_ref, f_ref, g_ref, b_ref, o_ref, ob_ref, *, alpha):
    z = alpha * x_ref[...] + f_ref[...]
    mu = jnp.mean(z, -1, keepdims=True)
    zc = z - mu
    var = jnp.mean(zc * zc, -1, keepdims=True)
    y = zc * lax.rsqrt(var + LN_EPS) * g_ref[...] + b_ref[...]
    o_ref[...] = y
    ob_ref[...] = y.astype(BF16)


def _res_ln(x, f, g, b, *, alpha, tm, name="res_ln"):
    m, d = x.shape
    assert m % tm == 0
    row = pl.BlockSpec((tm, d), lambda i: (i, 0))
    vec = pl.BlockSpec((1, d), lambda i: (0, 0))
    return pl.pallas_call(
        functools.partial(_ln_body, alpha=alpha),
        grid=(m // tm,),
        in_specs=[row, row, vec, vec],
        out_specs=[row, row],
        out_shape=[jax.ShapeDtypeStruct((m, d), F32), jax.ShapeDtypeStruct((m, d), BF16)],
        compiler_params=_cparams(("parallel",)),
        name=name,
    )(x, f, g.reshape(1, d), b.reshape(1, d))


def _combine_ln_body(x_ref, ya_ref, yb_ref, gt_ref, g_ref, b_ref, o_ref, *, alpha):
    gt = gt_ref[...]
    f = ya_ref[...] * gt[:, 0:1] + yb_ref[...] * gt[:, 1:2]
    z = alpha * x_ref[...] + f
    mu = jnp.mean(z, -1, keepdims=True)
    zc = z - mu
    var = jnp.mean(zc * zc, -1, keepdims=True)
    o_ref[...] = zc * lax.rsqrt(var + LN_EPS) * g_ref[...] + b_ref[...]


def _combine_ln(x, ya, yb, gates, g, b, *, alpha, tm, name="moe_combine_ln"):
    m, d = x.shape
    assert m % tm == 0
    row = pl.BlockSpec((tm, d), lambda i: (i, 0))
    vec = pl.BlockSpec((1, d), lambda i: (0, 0))
    return pl.pallas_call(
        functools.partial(_combine_ln_body, alpha=alpha),
        grid=(m // tm,),
        in_specs=[row, row, row, pl.BlockSpec((tm, gates.shape[1]), lambda i: (i, 0)), vec, vec],
        out_specs=row,
        out_shape=jax.ShapeDtypeStruct((m, d), F32),
        compiler_params=_cparams(("parallel",)),
        name=name,
    )(x, ya, yb, gates, g.reshape(1, d), b.reshape(1, d))


def _wkv_body(r_ref, w_ref, k_ref, v_ref, a_ref, b_ref, s0_ref, y_ref, s_ref, *, tc, n):
    t_blk = pl.program_id(1)

    @pl.when(t_blk == 0)
    def _init():
        s_ref[...] = s0_ref[...]

    def row(ref, t, j):
        return ref[t, pl.ds(j, 1), :]

    def step(t, carry):
        sa = jnp.zeros((n, LANES), F32)
        for j in range(n):
            sa = sa + s_ref[j] * row(a_ref, t, j)
        v_t = v_ref[t]
        y = jnp.zeros((n, LANES), F32)
        for j in range(n):
            s_new = s_ref[j] * row(w_ref, t, j) + sa * row(b_ref, t, j) + v_t * row(k_ref, t, j)
            s_ref[j] = s_new
            y = y + s_new * row(r_ref, t, j)
        y_ref[t] = y
        return carry

    lax.fori_loop(0, tc, step, 0)


def _wkv_scan(r, w, k, v, a, b, s0, *, tc, name="wkv_scan"):
    t_len, n, lanes = r.shape
    assert lanes % LANES == 0 and t_len % tc == 0
    seq = pl.BlockSpec((tc, n, LANES), lambda l, t: (t, 0, l))
    st = pl.BlockSpec((n, n, LANES), lambda l, t: (0, 0, l))
    return pl.pallas_call(
        functools.partial(_wkv_body, tc=tc, n=n),
        grid=(lanes // LANES, t_len // tc),
        in_specs=[seq] * 6 + [st],
        out_specs=[seq, st],
        out_shape=[jax.ShapeDtypeStruct((t_len, n, lanes), F32),
                   jax.ShapeDtypeStruct((n, n, lanes), F32)],
        compiler_params=_cparams(("parallel", "arbitrary")),
        name=name,
    )(r, w, k, v, a, b, s0)


CHUNK = 128
SUB = 32


def _head_normalize(x_t, hl, n):
    x3 = x_t.reshape(hl, n, CHUNK)
    nrm = jnp.sqrt(jnp.sum(x3 * x3, axis=1, keepdims=True))
    return (x3 / jnp.maximum(nrm, 1e-12)).reshape(hl * n, CHUNK)


def _to_scan_body(*refs, mode, nb, hl, n):
    n_in = {"copy": 1, "k2": 2, "kk": 1, "kb": 2}[mode]
    tiles = [refs[i * nb:(i + 1) * nb] for i in range(n_in)]
    pos = n_in * nb
    par_ref = None
    if mode != "copy":
        par_ref = refs[pos]
        pos += 1
    o_ref, sc_ref = refs[pos], refs[pos + 1]

    for b in range(nb):
        if mode == "copy":
            res = tiles[0][b][...].T
        elif mode == "k2":
            k, a = tiles[0][b][...], tiles[1][b][...]
            res = (k * (1.0 + (a - 1.0) * par_ref[...])).T
        elif mode == "kk":
            res = _head_normalize((tiles[0][b][...] * par_ref[...]).T, hl, n)
        else:
            kk = _head_normalize((tiles[0][b][...] * par_ref[...]).T, hl, n)
            res = kk * tiles[1][b][...].T
        sc_ref[b] = res

    for j in range(n):
        slab = jnp.concatenate([sc_ref.at[b][pl.ds(j, hl, stride=n), :] for b in range(nb)], axis=0)
        slab_t = slab.T
        for s in range(CHUNK // SUB):
            o_ref[s, pl.ds(j * SUB, SUB), :] = slab_t[s * SUB:(s + 1) * SUB, :]


def _to_scan(inputs, par, *, mode, nb, seq, heads, n, name):
    hl = LANES // nb
    width = hl * n
    n_hh = heads // hl
    n_c = seq // CHUNK
    in_specs, args = [], []
    for x in inputs:
        for b in range(nb):
            in_specs.append(pl.BlockSpec((CHUNK, width), lambda hh, c, b=b: (b * n_c + c, hh)))
            args.append(x)
    if par is not None:
        in_specs.append(pl.BlockSpec((1, width), lambda hh, c: (0, hh)))
        args.append(par.reshape(1, heads * n))
    return pl.pallas_call(
        functools.partial(_to_scan_body, mode=mode, nb=nb, hl=hl, n=n),
        grid=(n_hh, n_c),
        in_specs=in_specs,
        out_specs=pl.BlockSpec((CHUNK // SUB, n * SUB, LANES), lambda hh, c: (c, 0, hh)),
        out_shape=jax.ShapeDtypeStruct((seq // SUB, n * SUB, n_hh * LANES), F32),
        scratch_shapes=[pltpu.VMEM((nb, width, CHUNK), F32)],
        compiler_params=_cparams(("parallel", "parallel")),
        name=name,
    )(*args)


def _wkv_sl_body(r_ref, w_ref, k_ref, v_ref, kk_ref, b_ref, y_ref, s_ref, *, n):
    @pl.when(pl.program_id(1) == 0)
    def _init():
        s_ref[...] = jnp.zeros_like(s_ref)

    def row(ref, t, j):
        return ref[pl.ds(j * SUB + t, 1), :]

    def step(t, skk):
        t_next = jnp.minimum(t + 1, SUB - 1)
        v_t = v_ref[pl.ds(t, n, stride=SUB), :]
        y = jnp.zeros((n, LANES), F32)
        skk_next = jnp.zeros((n, LANES), F32)
        for j in range(n):
            s_new = s_ref[j] * row(w_ref, t, j) - skk * row(b_ref, t, j) + v_t * row(k_ref, t, j)
            s_ref[j] = s_new
            y = y + s_new * row(r_ref, t, j)
            skk_next = skk_next + s_new * row(kk_ref, t_next, j)
        y_ref[pl.ds(t, n, stride=SUB), :] = y
        return skk_next

    skk0 = jnp.zeros((n, LANES), F32)
    for j in range(n):
        skk0 = skk0 + s_ref[j] * row(kk_ref, 0, j)
    lax.fori_loop(0, SUB, step, skk0)


def _wkv_scan_sl(r, w, k, v, kk, b, *, n, name="wkv_scan_prompt"):
    n_blk, rows, lanes = r.shape
    seq = pl.BlockSpec((None, rows, LANES), lambda l, t: (t, 0, l))
    st = pl.BlockSpec((n, n, LANES), lambda l, t: (0, 0, l))
    return pl.pallas_call(
        functools.partial(_wkv_sl_body, n=n),
        grid=(lanes // LANES, n_blk),
        in_specs=[seq] * 6,
        out_specs=[seq, st],
        out_shape=[jax.ShapeDtypeStruct((n_blk, rows, lanes), F32),
                   jax.ShapeDtypeStruct((n, n, lanes), F32)],
        compiler_params=_cparams(("parallel", "arbitrary")),
        name=name,
    )(r, w, k, v, kk, b)


def _from_scan_body(y_ref, r_ref, k_ref, v_ref, rk_ref, lg_ref, lb_ref, g_ref, o_ref, z_ref, sc_ref,
                    *, nb, hl, n, eps):
    b = pl.program_id(2)

    @pl.when(b == 0)
    def _fill():
        for s in range(CHUNK // SUB):
            def blk(ref, j, s=s):
                return ref[s, pl.ds(j * SUB, SUB), :]

            coef = jnp.zeros((SUB, LANES), F32)
            tot = jnp.zeros((SUB, LANES), F32)
            for j in range(n):
                coef = coef + blk(r_ref, j) * blk(k_ref, j) * rk_ref[pl.ds(j, 1), :]
                tot = tot + blk(y_ref, j)
            mu = tot / n
            var = jnp.zeros((SUB, LANES), F32)
            for j in range(n):
                d = blk(y_ref, j) - mu
                var = var + d * d
            inv = lax.rsqrt(var / n + eps)
            for j in range(n):
                z = (blk(y_ref, j) - mu) * inv * lg_ref[pl.ds(j, 1), :] + lb_ref[pl.ds(j, 1), :]
                z_ref[j, pl.ds(s * SUB, SUB), :] = z + coef * blk(v_ref, j)
        for j in range(n):
            slab_t = z_ref[j].T
            for bb in range(nb):
                sc_ref.at[bb][pl.ds(j, hl, stride=n), :] = slab_t[bb * hl:(bb + 1) * hl, :]

    o_ref[...] = (sc_ref[b].T * g_ref[...]).astype(o_ref.dtype)


def _from_scan(y, r, k2, v, rk_t, lg_t, lb_t, g, *, nb, seq, heads, n, eps, name="rwkv_post"):
    hl = LANES // nb
    width = hl * n
    n_hh = heads // hl
    n_c = seq // CHUNK
    sl = pl.BlockSpec((CHUNK // SUB, n * SUB, LANES), lambda hh, c, b: (c, 0, hh))
    par = pl.BlockSpec((n, LANES), lambda hh, c, b: (0, hh))
    tok = pl.BlockSpec((CHUNK, width), lambda hh, c, b: (b * n_c + c, hh))
    return pl.pallas_call(
        functools.partial(_from_scan_body, nb=nb, hl=hl, n=n, eps=eps),
        grid=(n_hh, n_c, nb),
        in_specs=[sl, sl, sl, sl, par, par, par, tok],
        out_specs=tok,
        out_shape=jax.ShapeDtypeStruct((nb * seq, heads * n), BF16),
        scratch_shapes=[pltpu.VMEM((n, CHUNK, LANES), F32), pltpu.VMEM((nb, width, CHUNK), F32)],
        compiler_params=_cparams(("parallel", "parallel", "arbitrary")),
        name=name,
    )(y, r, k2, v, rk_t, lg_t, lb_t, g)


def _head_param_lanes(p, nb, heads, n):
    hl = LANES // nb
    t = p.reshape(heads // hl, 1, hl, n)
    t = jnp.broadcast_to(t, (heads // hl, nb, hl, n))
    return jnp.transpose(t, (3, 0, 1, 2)).reshape(n, (heads // hl) * LANES).astype(F32)


SPAN = 128
ATTN_UNROLL = 5


def _attn_block(q, k, v, causal_own, scale):
    s = lax.dot_general(q.astype(BF16), k.astype(BF16), (((1,), (1,)), ((), ())),
                        preferred_element_type=F32) * scale
    qi = lax.broadcasted_iota(jnp.int32, s.shape, 0)
    ki = lax.broadcasted_iota(jnp.int32, s.shape, 1)
    if causal_own:
        mask = ki <= qi
    else:
        mask = (ki >= qi) & (ki <= qi + SPAN)
    s = jnp.where(mask, s, -jnp.inf)
    m = jnp.max(s, axis=1, keepdims=True)
    p = jnp.exp(s - m)
    l = jnp.sum(p, axis=1, keepdims=True)
    o = jnp.dot(p.astype(BF16), v.astype(BF16), preferred_element_type=F32)
    return o, m, l


def _dilated_attn_body(*refs, dils, seq, scale):
    ng = len(dils)
    q_refs, k_refs, v_refs = refs[:ng], refs[ng:2 * ng], refs[2 * ng:3 * ng]
    o_ref = refs[3 * ng]
    scratch = refs[3 * ng + 1:]
    acc_refs, m_refs, l_refs = scratch[:ng], scratch[ng:2 * ng], scratch[2 * ng:]

    for g, dil in enumerate(dils):
        q_ref, k_ref, v_ref = q_refs[g], k_refs[g], v_refs[g]
        acc_ref, m_ref, l_ref = acc_refs[g], m_refs[g], l_refs[g]
        nblk = seq // (dil * SPAN)

        def rows(start, size, dil=dil):
            if dil == 1:
                return pl.ds(start, size)
            return pl.ds(start, size, stride=dil)

        def put(sl, res, acc_ref=acc_ref, m_ref=m_ref, l_ref=l_ref):
            o, m, l = res
            acc_ref[sl, :] = o
            m_ref[sl, :] = m
            l_ref[sl, :] = l

        for r in range(dil):
            sl0 = rows(r, SPAN)
            put(sl0, _attn_block(q_ref[sl0, :], k_ref[sl0, :], v_ref[sl0, :], True, scale))

            def body(nb, carry, r=r, dil=dil, rows=rows, put=put, q_ref=q_ref, k_ref=k_ref, v_ref=v_ref):
                q_sl = rows(r + dil * SPAN * nb, SPAN)
                kv_sl = rows(r + dil * SPAN * (nb - 1), 2 * SPAN)
                put(q_sl, _attn_block(q_ref[q_sl, :], k_ref[kv_sl, :], v_ref[kv_sl, :], False, scale))
                return carry

            if nblk > 1:
                lax.fori_loop(1, nblk, body, 0, unroll=ATTN_UNROLL)

    m_all = m_refs[0][...]
    for g in range(1, ng):
        m_all = jnp.maximum(m_all, m_refs[g][...])
    num = jnp.zeros(acc_refs[0].shape, F32)
    den = jnp.zeros(m_all.shape, F32)
    for g in range(ng):
        e = jnp.exp(m_refs[g][...] - m_all)
        num = num + e * acc_refs[g][...]
        den = den + e * l_refs[g][...]
    o_ref[...] = (num / den).astype(o_ref.dtype)


def _dilated_attn(q_hm, k_hm, v_hm, *, batch, seq, heads, dils, v_head0=0, name="dilated_attn"):
    e = q_hm.shape[-1]
    ng = len(dils)

    def spec(g, h0=0):
        return pl.BlockSpec((None, seq, e), lambda b, h, g=g: (h0 + g * heads + h, b, 0))

    specs = [spec(g) for g in range(ng)]
    return pl.pallas_call(
        functools.partial(_dilated_attn_body, dils=dils, seq=seq, scale=e ** -0.5),
        grid=(batch, heads),
        in_specs=specs * 2 + [spec(g, v_head0) for g in range(ng)],
        out_specs=pl.BlockSpec((seq, e), lambda b, h: (b, h)),
        out_shape=jax.ShapeDtypeStruct((batch * seq, heads * e), BF16),
        scratch_shapes=([pltpu.VMEM((seq, e), F32)] * ng + [pltpu.VMEM((seq, 1), F32)] * (2 * ng)),
        compiler_params=_cparams(("parallel", "parallel")),
        name=name,
    )(*([q_hm] * ng + [k_hm] * ng + [v_hm] * ng))


def _sample_attn_body(q_ref, kn_ref, vn_ref, *rest, ng, heads, scale):
    cache_refs, o_ref = rest[:ng], rest[ng]
    for h in range(heads):
        parts = []
        for g in range(ng):
            row = g * heads + h
            q = q_ref[pl.ds(row, 1), :]
            kc = cache_refs[g][:, 0, h, :]
            vc = cache_refs[g][:, 1, h, :]
            kn, vn = kn_ref[pl.ds(row, 1), :], vn_ref[pl.ds(row, 1), :]
            qb = q.astype(BF16)
            s = lax.dot_general(qb, kc.astype(BF16), (((1,), (1,)), ((), ())),
                                preferred_element_type=F32) * scale
            s_n = jnp.sum(qb.astype(F32) * kn.astype(BF16).astype(F32), axis=1, keepdims=True) * scale
            m = jnp.maximum(jnp.max(s, axis=1, keepdims=True), s_n)
            p, p_n = jnp.exp(s - m), jnp.exp(s_n - m)
            l = jnp.sum(p, axis=1, keepdims=True) + p_n
            o = jnp.dot(p.astype(BF16), vc.astype(BF16), preferred_element_type=F32)
            o = o + p_n.astype(BF16).astype(F32) * vn.astype(BF16).astype(F32)
            parts.append((o, m, l))
        m_all = parts[0][1]
        for _, m, _ in parts[1:]:
            m_all = jnp.maximum(m_all, m)
        num = jnp.zeros_like(parts[0][0])
        den = jnp.zeros_like(m_all)
        for o, m, l in parts:
            e = jnp.exp(m - m_all)
            num = num + e * o
            den = den + e * l
        o_ref[pl.ds(h, 1), :] = (num / den).astype(o_ref.dtype)


def _sample_attn(q_s, k_new, v_new, caches, dils, *, heads, name="sample_attn"):
    bs, _, e = q_s.shape
    ng = len(dils)
    tok = pl.BlockSpec((None, ng * heads, e), lambda b: (b, 0, 0))
    in_specs, args = [tok, tok, tok], [q_s, k_new, v_new]
    for c, dil in zip(caches, dils):
        w = c.shape[1]
        assert w == SPAN * dil
        args.append(c.reshape(bs, SPAN, dil, 2, heads, e))
        in_specs.append(pl.BlockSpec((None, SPAN, None, 2, heads, e), lambda b: (b, 0, 0, 0, 0, 0)))
    return pl.pallas_call(
        functools.partial(_sample_attn_body, ng=ng, heads=heads, scale=e ** -0.5),
        grid=(bs,),
        in_specs=in_specs,
        out_specs=pl.BlockSpec((None, heads, e), lambda b: (b, 0, 0)),
        out_shape=jax.ShapeDtypeStruct((bs, heads, e), BF16),
        compiler_params=_cparams(("parallel",)),
        name=name,
    )(*args)


def _router_body(x_ref, w_ref, o_ref):
    o_ref[...] = jnp.dot(x_ref[...], w_ref[...], precision=lax.Precision.HIGHEST,
                         preferred_element_type=F32)


def _router_logits(x, router, *, tm, name="router_logits"):
    m, d = x.shape
    n_exp = router.shape[1]
    w = jnp.zeros((d, LANES), F32).at[:, :n_exp].set(router.astype(F32))
    out = pl.pallas_call(
        _router_body,
        grid=(m // tm,),
        in_specs=[pl.BlockSpec((tm, d), lambda i: (i, 0)), pl.BlockSpec((d, LANES), lambda i: (0, 0))],
        out_specs=pl.BlockSpec((tm, LANES), lambda i: (i, 0)),
        out_shape=jax.ShapeDtypeStruct((m, LANES), F32),
        compiler_params=_cparams(("parallel",)),
        name=name,
    )(x, w)
    return out[:, :n_exp]


def _moe_in_body(be_ref, ok_ref, src_ref, x_ref, wg_ref, wu_ref, o_ref, wgb_ref, wub_ref):
    r = pl.program_id(1)
    changed = (r == 0) | (be_ref[r] != be_ref[jnp.maximum(r - 1, 0)])

    @pl.when(changed)
    def _cast():
        wgb_ref[...] = wg_ref[...].astype(BF16)
        wub_ref[...] = wu_ref[...].astype(BF16)

    @pl.when(ok_ref[r] != 0)
    def _compute():
        x = x_ref[...]
        g = jnp.dot(x, wgb_ref[...], preferred_element_type=F32)
        u = jnp.dot(x, wub_ref[...], preferred_element_type=F32)
        o_ref[...] = (g * jax.nn.sigmoid(g) * u).astype(o_ref.dtype)

    @pl.when(ok_ref[r] == 0)
    def _skip():
        o_ref[...] = jnp.zeros_like(o_ref)


def _moe_in(x_buf, w_in, blk_e, blk_ok, blk_src, *, tn, name="moe_swiglu_in"):
    rows, d = x_buf.shape
    n_exp, _, n2 = w_in.shape
    hid = n2 // 2
    nt = hid // tn
    rb = rows // MOE_ROWS
    grid_spec = pltpu.PrefetchScalarGridSpec(
        num_scalar_prefetch=3,
        grid=(nt, rb),
        in_specs=[pl.BlockSpec((MOE_ROWS, d), lambda j, r, be, ok, src: (src[r], 0)),
                  pl.BlockSpec((None, d, tn), lambda j, r, be, ok, src: (be[r], 0, j)),
                  pl.BlockSpec((None, d, tn), lambda j, r, be, ok, src: (be[r], 0, j + nt))],
        out_specs=pl.BlockSpec((MOE_ROWS, tn), lambda j, r, be, ok, src: (r, j)),
        scratch_shapes=[pltpu.VMEM((d, tn), BF16), pltpu.VMEM((d, tn), BF16)],
    )
    return pl.pallas_call(
        _moe_in_body,
        grid_spec=grid_spec,
        out_shape=jax.ShapeDtypeStruct((rows, hid), BF16),
        compiler_params=_cparams(("arbitrary", "arbitrary")),
        name=name,
    )(blk_e, blk_ok, blk_src, x_buf, w_in, w_in)


def _moe_out_body(pe_ref, nv_ref, src_ref, h_ref, w_ref, o_ref, acc_ref, *, nk):
    p = pl.program_id(0)
    k = pl.program_id(2)
    nv = nv_ref[p]

    @pl.when(k == 0)
    def _init():
        acc_ref[...] = jnp.zeros_like(acc_ref)

    @pl.when(nv == 2)
    def _both():
        acc_ref[...] += jnp.dot(h_ref[...], w_ref[...].astype(BF16), preferred_element_type=F32)

    @pl.when(nv == 1)
    def _first():
        acc_ref[:MOE_ROWS, :] += jnp.dot(h_ref[:MOE_ROWS, :], w_ref[...].astype(BF16),
                                         preferred_element_type=F32)

    @pl.when(k == nk - 1)
    def _fin():
        o_ref[...] = acc_ref[...]


def _moe_out(h_buf, w_out, pair_e, pair_nv, pair_src, *, tn, tk, name="moe_out"):
    rows, hid = h_buf.shape
    n_exp, _, d = w_out.shape
    n_pair = rows // (2 * MOE_ROWS)
    nk = hid // tk

    def k_eff(p, k, nv):
        return jnp.where(nv[p] > 0, k, nk - 1)

    grid_spec = pltpu.PrefetchScalarGridSpec(
        num_scalar_prefetch=3,
        grid=(n_pair, d // tn, nk),
        in_specs=[pl.BlockSpec((2 * MOE_ROWS, tk), lambda p, j, k, pe, nv, src: (src[p], k_eff(p, k, nv))),
                  pl.BlockSpec((None, tk, tn), lambda p, j, k, pe, nv, src: (pe[p], k_eff(p, k, nv), j))],
        out_specs=pl.BlockSpec((2 * MOE_ROWS, tn), lambda p, j, k, pe, nv, src: (p, j)),
        scratch_shapes=[pltpu.VMEM((2 * MOE_ROWS, tn), F32)],
    )
    return pl.pallas_call(
        functools.partial(_moe_out_body, nk=nk),
        grid_spec=grid_spec,
        out_shape=jax.ShapeDtypeStruct((rows, d), F32),
        compiler_params=_cparams(("arbitrary", "arbitrary", "arbitrary")),
        name=name,
    )(pair_e, pair_nv, pair_src, h_buf, w_out)


def _moe(x_f32, n_tok, router, w_in, w_out):
    rows, d = x_f32.shape
    n_exp = router.shape[1]
    top_k = 2
    logits = _router_logits(x_f32, router, tm=520)[:n_tok]
    top_v, top_i = lax.top_k(logits, top_k)
    gates = jax.nn.softmax(top_v, axis=-1)
    n_asg = n_tok * top_k
    e = top_i.reshape(n_asg).astype(jnp.int32)
    tok = jnp.repeat(jnp.arange(n_tok, dtype=jnp.int32), top_k)
    order = jnp.argsort(e)
    e_s, tok_s = e[order], tok[order]
    counts = jnp.bincount(e, length=n_exp).astype(jnp.int32)
    starts = jnp.cumsum(counts) - counts
    padded = -(-counts // MOE_ROWS) * MOE_ROWS
    region = -(-padded // (2 * MOE_ROWS)) * (2 * MOE_ROWS)
    region_ends = jnp.cumsum(region)
    region_starts = region_ends - region
    dest = (region_starts[e_s] + jnp.arange(n_asg, dtype=jnp.int32) - starts[e_s]).astype(jnp.int32)
    n_pair = (-(-n_asg // MOE_ROWS) + 2 * n_exp + 1) // 2
    n_blk = 2 * n_pair
    src_tok = jnp.zeros((n_blk * MOE_ROWS,), jnp.int32).at[dest].set(tok_s)
    pos = jnp.zeros((n_asg,), jnp.int32).at[order].set(dest).reshape(n_tok, top_k)
    blk_start = jnp.arange(n_blk, dtype=jnp.int32) * MOE_ROWS
    blk_e = jnp.minimum(jnp.searchsorted(region_ends, blk_start, side='right'), n_exp - 1).astype(jnp.int32)
    blk_ok = ((blk_start - region_starts[blk_e]) < padded[blk_e]).astype(jnp.int32)
    blk_src = lax.cummax(jnp.where(blk_ok > 0, jnp.arange(n_blk, dtype=jnp.int32), 0))
    pair_nv = blk_ok.reshape(n_pair, 2).sum(axis=1).astype(jnp.int32)
    pair_src = lax.cummax(jnp.where(pair_nv > 0, jnp.arange(n_pair, dtype=jnp.int32), 0))

    blk_ids = jnp.arange(n_blk, dtype=jnp.int32)
    blk_next = lax.cummin(jnp.where(blk_ok > 0, blk_ids, n_blk - 1), axis=0, reverse=True)
    x_buf = x_f32[src_tok].astype(BF16)
    h_buf = _moe_in(x_buf, w_in, blk_e[blk_next], blk_ok, blk_src, tn=512)
    y_buf = _moe_out(h_buf, w_out, blk_e[::2], pair_nv, pair_src, tn=min(2048, d), tk=1024)
    pos = jnp.zeros((rows, top_k), jnp.int32).at[:n_tok].set(pos)
    gates = jnp.zeros((rows, top_k), F32).at[:n_tok].set(gates)
    return y_buf[pos[:, 0]], y_buf[pos[:, 1]], gates


def _rope_tables(pos, hd):
    inv = ROPE_THETA ** (-jnp.arange(0, hd, 2, dtype=F32) / hd)
    ang = pos.astype(F32)[:, None] * inv[None, :]
    ang = jnp.concatenate([ang, ang], -1)
    return jnp.cos(ang), jnp.sin(ang)


def _softplus(z):
    return jnp.maximum(z, 0.0) + jnp.log(1.0 + jnp.exp(-jnp.abs(z)))


def _decay_epilogue(y):
    return jnp.exp(-jnp.exp(-_softplus(-y) - 0.5))


def kernel(x_prompt, x_sample, state_wkv, state_shift, cache_kv_w128, cache_kv_w512, cache_kv_w2048,
           ln_g, ln_b, rwkv_mix, rwkv_w_rkv, rwkv_w0, rwkv_w1, rwkv_w2, rwkv_a0, rwkv_a1, rwkv_a2,
           rwkv_g1, rwkv_g2, rwkv_k_k, rwkv_k_a, rwkv_r_k, rwkv_lnx_g, rwkv_lnx_b, rwkv_w_out,
           attn_w_kv, attn_w_q, attn_w_out, ffn_w_in, ffn_w_out, moe_router, moe_w_in, moe_w_out):
    bp, seq, d = x_prompt.shape
    bs, seq_s, _ = x_sample.shape
    depth = ln_g.shape[0]
    assert depth == 2 and seq_s == 1
    heads_r, hd_r = rwkv_r_k.shape[1], rwkv_r_k.shape[2]
    past_len = PAST_LEN
    groups = ((128, 1), (512, 4), (2048, 16))
    ng = len(groups)
    hd_a = 128
    heads_a = attn_w_out.shape[1] // hd_a
    alpha = (2.0 * depth) ** 0.25
    lnx_eps = 1e-5 * hd_r

    n_p = bp * seq
    n_tok = n_p + bs
    tm = 2080
    rows = -(-n_tok // tm) * tm

    def pad_rows(a):
        return jnp.concatenate([a, jnp.zeros((rows - a.shape[0],) + a.shape[1:], a.dtype)], 0)

    x0 = pad_rows(jnp.concatenate([x_prompt.reshape(n_p, d), x_sample.reshape(bs, d)], 0))
    prev_p = jnp.concatenate([jnp.zeros((bp, 1, d), F32), x_prompt[:, :-1]], 1).reshape(n_p, d)
    x_prev = pad_rows(jnp.concatenate([prev_p, state_shift[0]], 0))
    xx = x_prev - x0
    xs = [(x0 + xx * rwkv_mix[0, c]).astype(BF16) for c in range(6)]

    w_rkv = rwkv_w_rkv.reshape(3, d, d)
    mm = functools.partial(_matmul, tm=tm)
    r = mm(xs[0], w_rkv, 0, tn=1024, tk=1024, name="rwkv_r")
    k = mm(xs[1], w_rkv, 1, tn=1024, tk=1024, name="rwkv_k")
    v = mm(xs[2], w_rkv, 2, tn=1024, tk=1024, name="rwkv_v")
    lw, la, lg = rwkv_w1.shape[2], rwkv_a1.shape[2], rwkv_g1.shape[2]
    hw = mm(xs[3], rwkv_w1, 0, tn=lw, tk=1024, out_dtype=BF16, epilogue=jnp.tanh, name="rwkv_w1")
    decay = mm(hw, rwkv_w2, 0, tn=1024, tk=lw, bias=rwkv_w0[0], epilogue=_decay_epilogue, name="rwkv_w2")
    ha = mm(xs[4], rwkv_a1, 0, tn=la, tk=1024, out_dtype=BF16, name="rwkv_a1")
    a = mm(ha, rwkv_a2, 0, tn=1024, tk=la, bias=rwkv_a0[0], epilogue=jax.nn.sigmoid, name="rwkv_a2")
    hg = mm(xs[5], rwkv_g1, 0, tn=lg, tk=1024, out_dtype=BF16, epilogue=jax.nn.sigmoid, name="rwkv_g1")
    g = mm(hg, rwkv_g2, 0, tn=1024, tk=lg, name="rwkv_g2")

    hl = LANES // bp
    assert LANES % bp == 0 and heads_r % hl == 0 and seq % CHUNK == 0
    kw = dict(nb=bp, seq=seq, heads=heads_r, n=hd_r)
    r_sl = _to_scan([r], None, mode="copy", name="sl_r", **kw)
    w_sl = _to_scan([decay], None, mode="copy", name="sl_w", **kw)
    v_sl = _to_scan([v], None, mode="copy", name="sl_v", **kw)
    k_sl = _to_scan([k, a], rwkv_k_a[0], mode="k2", name="sl_k", **kw)
    kk_sl = _to_scan([k], rwkv_k_k[0], mode="kk", name="sl_kk", **kw)
    b_sl = _to_scan([k, a], rwkv_k_k[0], mode="kb", name="sl_b", **kw)
    y_sl, s_p = _wkv_scan_sl(r_sl, w_sl, k_sl, v_sl, kk_sl, b_sl, n=hd_r)
    tab = functools.partial(_head_param_lanes, nb=bp, heads=heads_r, n=hd_r)
    y_p = _from_scan(y_sl, r_sl, k_sl, v_sl, tab(rwkv_r_k.reshape(-1)), tab(rwkv_lnx_g[0]),
                     tab(rwkv_lnx_b[0]), g, eps=lnx_eps, **kw)
    s_p = s_p.reshape(hd_r, hd_r, heads_r // hl, bp, hl)
    prompt_wkv = jnp.transpose(s_p, (3, 2, 4, 1, 0)).reshape(1, bp, heads_r, hd_r, hd_r).astype(state_wkv.dtype)

    def hs(t):
        return t[n_p:n_tok].reshape(bs, heads_r, hd_r)

    kk_s = hs(k) * rwkv_k_k[0].reshape(heads_r, hd_r)
    kk_s = kk_s / jnp.maximum(jnp.linalg.norm(kk_s, axis=-1, keepdims=True), 1e-12)
    k2_s = hs(k) * (1.0 + (hs(a) - 1.0) * rwkv_k_a[0].reshape(heads_r, hd_r))
    bonus_s = jnp.sum(hs(r) * k2_s * rwkv_r_k[0], -1, keepdims=True) * hs(v)

    def lanes_s(t):
        return jnp.transpose(t, (2, 0, 1)).reshape(1, hd_r, bs * heads_r)

    scan_s = (hs(r), hs(decay), k2_s, hs(v), -kk_s, kk_s * hs(a))
    s0_s = jnp.transpose(state_wkv[0].astype(F32), (3, 2, 0, 1)).reshape(hd_r, hd_r, bs * heads_r)
    y_s, s_s = _wkv_scan(*[lanes_s(t) for t in scan_s], s0_s, tc=1, name="wkv_scan_sample")
    y_s = jnp.transpose(y_s.reshape(hd_r, bs, heads_r), (1, 2, 0))
    sample_wkv = jnp.transpose(s_s.reshape(hd_r, hd_r, bs, heads_r), (2, 3, 1, 0))[None].astype(state_wkv.dtype)
    mu = jnp.mean(y_s, -1, keepdims=True)
    var = jnp.mean(jnp.square(y_s - mu), -1, keepdims=True)
    y_s = ((y_s - mu) * lax.rsqrt(var + lnx_eps)).reshape(bs, d) * rwkv_lnx_g[0] + rwkv_lnx_b[0]
    y_s = ((y_s + bonus_s.reshape(bs, d)) * g[n_p:n_tok]).astype(BF16)

    y = pad_rows(jnp.concatenate([y_p, y_s], 0))
    mix0 = mm(y, rwkv_w_out, 0, tn=1024, tk=1024, name="rwkv_out")
    x1, x1b = _res_ln(x0, mix0, ln_g[0, 0], ln_b[0, 0], alpha=alpha, tm=208, name="ln_0a")

    h = _swiglu_in(x1b, ffn_w_in, 0, tm=tm, tn=256, tk=2048, name="ffn_in")
    f0 = mm(h, ffn_w_out, 0, tn=1024, tk=1024, name="ffn_out")
    x2, x2b = _res_ln(x1, f0, ln_g[0, 1], ln_b[0, 1], alpha=alpha, tm=208, name="ln_0b")

    pos = jnp.concatenate([jnp.tile(jnp.arange(seq, dtype=jnp.int32), bp),
                           jnp.full((bs,), past_len, jnp.int32),
                           jnp.zeros((rows - n_tok,), jnp.int32)])
    cos, sin = _rope_tables(pos, hd_a)
    half_sign = jnp.concatenate([-jnp.ones((hd_a // 2,), F32), jnp.ones((hd_a // 2,), F32)])
    sin_s = sin * half_sign
    n_qh = ng * heads_a
    tn_h = 1024
    kv_hm = _matmul_heads(x2b, attn_w_kv.reshape(1, d, -1), 0, cos, sin_s, n_rope=n_qh * hd_a // tn_h,
                          tm=tm, tn=tn_h, tk=1024, hd=hd_a, name="attn_kv")
    q_hm = _matmul_heads(x2b, attn_w_q, 0, cos, sin_s, n_rope=n_qh * hd_a // tn_h,
                         tm=tm, tn=tn_h, tk=1024, hd=hd_a, name="attn_q")
    dils = tuple(dl for _, dl in groups)
    att_p = _dilated_attn(q_hm, kv_hm, kv_hm, batch=bp, seq=seq, heads=heads_a, dils=dils, v_head0=n_qh)

    caches = (cache_kv_w128, cache_kv_w512, cache_kv_w2048)
    q_s = jnp.transpose(q_hm[:, n_p:n_tok], (1, 0, 2))
    kv_s = jnp.transpose(kv_hm[:, n_p:n_tok], (1, 0, 2)).reshape(bs, 2, n_qh, hd_a)
    att_s = _sample_attn(q_s, kv_s[:, 0], kv_s[:, 1], caches, dils, heads=heads_a)
    att = pad_rows(jnp.concatenate([att_p, att_s.reshape(bs, heads_a * hd_a)], 0))
    mix1 = mm(att, attn_w_out, 0, tn=1024, tk=1024, name="attn_out")
    x3, x3b = _res_ln(x2, mix1, ln_g[1, 0], ln_b[1, 0], alpha=alpha, tm=208, name="ln_1a")

    n_exp = moe_router.shape[-1]
    ya, yb, gates = _moe(x3, n_tok, moe_router.reshape(d, n_exp), moe_w_in.reshape(n_exp, d, -1),
                         moe_w_out.reshape(n_exp, -1, d))
    x4 = _combine_ln(x3, ya, yb, gates, ln_g[1, 1], ln_b[1, 1], alpha=alpha, tm=208)

    y_prompt = x4[:n_p].reshape(bp, seq, d)
    y_sample = x4[n_p:n_tok].reshape(bs, 1, d)
    prompt_shift = x_prompt[:, -1][None]
    sample_shift = x_sample[:, -1][None]
    prompt_kv, sample_kv = [], []
    kv_n = kv_s.reshape(bs, 1, 2, ng, heads_a, hd_a)
    for gi, (win, dil) in enumerate(groups):
        keep = min(win, seq)
        rows_g = _kv_window(kv_hm, group=gi, n_groups=ng, heads=heads_a, batch=bp, seq=seq, keep=keep,
                            name="kv_window_%d" % win)
        prompt_kv.append(rows_g.reshape(bp, keep, 2, heads_a, hd_a))
        sample_kv.append(kv_n[:, :, :, gi])
    return (y_prompt, y_sample, prompt_wkv, prompt_shift, prompt_kv[0], prompt_kv[1], prompt_kv[2],
            sample_wkv, sample_shift, sample_kv[0], sample_kv[1], sample_kv[2])
```

```python
import functools

import jax
import jax.numpy as jnp
from jax import lax
from jax.experimental import pallas as pl
from jax.experimental.pallas import tpu as pltpu

F32 = jnp.float32
BF16 = jnp.bfloat16

LANES = 128
SUBLANES = 8
VMEM_LIMIT = 56 * 1024 * 1024

LN_EPS = 1e-5
ROPE_THETA = 10000.0
PAST_LEN = 16384
MOE_ROWS = 512


def _cparams(sem):
    return pltpu.CompilerParams(dimension_semantics=sem, vmem_limit_bytes=VMEM_LIMIT)


def _mm_body(x_ref, w_ref, *rest, nk, k_rem, epilogue, has_bias):
    if has_bias:
        b_ref, o_ref, acc_ref = rest
    else:
        o_ref, acc_ref = rest
    k = pl.program_id(2)

    @pl.when(k == 0)
    def _init():
        acc_ref[...] = jnp.zeros_like(acc_ref)

    def accumulate(x, w):
        acc_ref[...] += jnp.dot(x.astype(BF16), w.astype(BF16), preferred_element_type=F32)

    if k_rem == 0:
        accumulate(x_ref[...], w_ref[...])
    else:
        @pl.when(k < nk - 1)
        def _full():
            accumulate(x_ref[...], w_ref[...])

        @pl.when(k == nk - 1)
        def _edge():
            x, w = x_ref[...], w_ref[...]
            col = lax.broadcasted_iota(jnp.int32, x.shape, 1)
            row = lax.broadcasted_iota(jnp.int32, w.shape, 0)
            accumulate(jnp.where(col < k_rem, x, jnp.zeros_like(x)),
                       jnp.where(row < k_rem, w, jnp.zeros_like(w)))

    @pl.when(k == nk - 1)
    def _fin():
        y = acc_ref[...]
        if has_bias:
            y = y + b_ref[...]
        if epilogue is not None:
            y = epilogue(y)
        o_ref[...] = y.astype(o_ref.dtype)


def _matmul(x, w3, li, *, tm, tn, tk, out_dtype=F32, bias=None, epilogue=None, name="mm"):
    m, kdim = x.shape
    _, kw, n = w3.shape
    assert kw == kdim and m % tm == 0 and n % tn == 0
    nk = pl.cdiv(kdim, tk)
    in_specs = [pl.BlockSpec((tm, tk), lambda i, j, k: (i, k)),
                pl.BlockSpec((None, tk, tn), lambda i, j, k: (li, k, j))]
    args = [x, w3]
    if bias is not None:
        in_specs.append(pl.BlockSpec((1, tn), lambda i, j, k: (0, j)))
        args.append(bias.reshape(1, n).astype(F32))
    return pl.pallas_call(
        functools.partial(_mm_body, nk=nk, k_rem=kdim % tk, epilogue=epilogue, has_bias=bias is not None),
        grid=(m // tm, n // tn, nk),
        in_specs=in_specs,
        out_specs=pl.BlockSpec((tm, tn), lambda i, j, k: (i, j)),
        out_shape=jax.ShapeDtypeStruct((m, n), out_dtype),
        scratch_shapes=[pltpu.VMEM((tm, tn), F32)],
        compiler_params=_cparams(("parallel", "parallel", "arbitrary")),
        name=name,
    )(*args)


def _mm_heads_body(x_ref, w_ref, cos_ref, sin_ref, o_ref, acc_ref, *, nk, n_rope, hd):
    j = pl.program_id(1)
    k = pl.program_id(2)

    @pl.when(k == 0)
    def _init():
        acc_ref[...] = jnp.zeros_like(acc_ref)

    acc_ref[...] += jnp.dot(x_ref[...].astype(BF16), w_ref[...].astype(BF16),
                            preferred_element_type=F32)
    n_heads = acc_ref.shape[1] // hd

    @pl.when((k == nk - 1) & (j < n_rope))
    def _fin_rope():
        cos, sin = cos_ref[...], sin_ref[...]
        for h in range(n_heads):
            y = acc_ref[:, h * hd:(h + 1) * hd]
            o_ref[h] = y * cos + pltpu.roll(y, hd // 2, 1) * sin

    @pl.when((k == nk - 1) & (j >= n_rope))
    def _fin_plain():
        for h in range(n_heads):
            o_ref[h] = acc_ref[:, h * hd:(h + 1) * hd]


def _matmul_heads(x, w3, li, cos, sin_s, *, n_rope, tm, tn, tk, hd, name):
    m, kdim = x.shape
    _, _, n = w3.shape
    assert m % tm == 0 and n % tn == 0 and kdim % tk == 0 and tn % hd == 0
    nk = kdim // tk
    hpt = tn // hd
    return pl.pallas_call(
        functools.partial(_mm_heads_body, nk=nk, n_rope=n_rope, hd=hd),
        grid=(m // tm, n // tn, nk),
        in_specs=[pl.BlockSpec((tm, tk), lambda i, j, k: (i, k)),
                  pl.BlockSpec((None, tk, tn), lambda i, j, k: (li, k, j)),
                  pl.BlockSpec((tm, hd), lambda i, j, k: (i, 0)),
                  pl.BlockSpec((tm, hd), lambda i, j, k: (i, 0))],
        out_specs=pl.BlockSpec((hpt, tm, hd), lambda i, j, k: (j, i, 0)),
        out_shape=jax.ShapeDtypeStruct((n // hd, m, hd), F32),
        scratch_shapes=[pltpu.VMEM((tm, tn), F32)],
        compiler_params=_cparams(("parallel", "parallel", "arbitrary")),
        name=name,
    )(x, w3, cos, sin_s)


def _window_body(x_ref, o_ref, *, heads, hd):
    for h in range(heads):
        o_ref[:, h * hd:(h + 1) * hd] = x_ref[h]


def _kv_window(kv_hm, *, group, n_groups, heads, batch, seq, keep, name):
    hd = kv_hm.shape[-1]
    tmc = min(keep, 1024)
    assert keep % tmc == 0 and (seq - keep) % tmc == 0
    per_b = keep // tmc

    def in_map(i, c):
        b, t = i // per_b, i % per_b
        return (c * n_groups + group, (b * seq + seq - keep) // tmc + t, 0)

    return pl.pallas_call(
        functools.partial(_window_body, heads=heads, hd=hd),
        grid=(batch * per_b, 2),
        in_specs=[pl.BlockSpec((heads, tmc, hd), in_map)],
        out_specs=pl.BlockSpec((tmc, heads * hd), lambda i, c: (i, c)),
        out_shape=jax.ShapeDtypeStruct((batch * keep, 2 * heads * hd), F32),
        compiler_params=_cparams(("parallel", "parallel")),
        name=name,
    )(kv_hm)


def _swiglu_body(x_ref, wg_ref, wu_ref, o_ref, accg_ref, accu_ref, *, nk):
    k = pl.program_id(2)

    @pl.when(k == 0)
    def _init():
        accg_ref[...] = jnp.zeros_like(accg_ref)
        accu_ref[...] = jnp.zeros_like(accu_ref)

    xb = x_ref[...].astype(BF16)
    accg_ref[...] += jnp.dot(xb, wg_ref[...].astype(BF16), preferred_element_type=F32)
    accu_ref[...] += jnp.dot(xb, wu_ref[...].astype(BF16), preferred_element_type=F32)

    @pl.when(k == nk - 1)
    def _fin():
        g = accg_ref[...]
        o_ref[...] = (g * jax.nn.sigmoid(g) * accu_ref[...]).astype(o_ref.dtype)


def _swiglu_in(x, w3, li, *, tm, tn, tk, name="swiglu_in"):
    m, kdim = x.shape
    _, _, n2 = w3.shape
    hid = n2 // 2
    assert m % tm == 0 and hid % tn == 0 and kdim % tk == 0
    nk = kdim // tk
    nt = hid // tn
    return pl.pallas_call(
        functools.partial(_swiglu_body, nk=nk),
        grid=(m // tm, nt, nk),
        in_specs=[pl.BlockSpec((tm, tk), lambda i, j, k: (i, k)),
                  pl.BlockSpec((None, tk, tn), lambda i, j, k: (li, k, j)),
                  pl.BlockSpec((None, tk, tn), lambda i, j, k: (li, k, j + nt))],
        out_specs=pl.BlockSpec((tm, tn), lambda i, j, k: (i, j)),
        out_shape=jax.ShapeDtypeStruct((m, hid), BF16),
        scratch_shapes=[pltpu.VMEM((tm, tn), F32), pltpu.VMEM((tm, tn), F32)],
        compiler_params=_cparams(("parallel", "parallel", "arbitrary")),
        name=name,
    )(x, w3, w3)


def _ln_body(x_ref, f_ref, g_ref, b_ref, o_ref, ob_ref, *, alpha):
    z = alpha * x_ref[...] + f_ref[...]
    mu = jnp.mean(z, -1, keepdims=True)
    zc = z - mu
    var = jnp.mean(zc * zc, -1, keepdims=True)
    y = zc * lax.rsqrt(var + LN_EPS) * g_ref[...] + b_ref[...]
    o_ref[...] = y
    ob_ref[...] = y.astype(BF16)


def _res_ln(x, f, g, b, *, alpha, tm, name="res_ln"):
    m, d = x.shape
    assert m % tm == 0
    row = pl.BlockSpec((tm, d), lambda i: (i, 0))
    vec = pl.BlockSpec((1, d), lambda i: (0, 0))
    return pl.pallas_call(
        functools.partial(_ln_body, alpha=alpha),
        grid=(m // tm,),
        in_specs=[row, row, vec, vec],
        out_specs=[row, row],
        out_shape=[jax.ShapeDtypeStruct((m, d), F32), jax.ShapeDtypeStruct((m, d), BF16)],
        compiler_params=_cparams(("parallel",)),
        name=name,
    )(x, f, g.reshape(1, d), b.reshape(1, d))


def _combine_ln_body(x_ref, ya_ref, yb_ref, gt_ref, g_ref, b_ref, o_ref, *, alpha):
    gt = gt_ref[...]
    f = ya_ref[...] * gt[:, 0:1] + yb_ref[...] * gt[:, 1:2]
    z = alpha * x_ref[...] + f
    mu = jnp.mean(z, -1, keepdims=True)
    zc = z - mu
    var = jnp.mean(zc * zc, -1, keepdims=True)
    o_ref[...] = zc * lax.rsqrt(var + LN_EPS) * g_ref[...] + b_ref[...]


def _combine_ln(x, ya, yb, gates, g, b, *, alpha, tm, name="moe_combine_ln"):
    m, d = x.shape
    assert m % tm == 0
    row = pl.BlockSpec((tm, d), lambda i: (i, 0))
    vec = pl.BlockSpec((1, d), lambda i: (0, 0))
    return pl.pallas_call(
        functools.partial(_combine_ln_body, alpha=alpha),
        grid=(m // tm,),
        in_specs=[row, row, row, pl.BlockSpec((tm, gates.shape[1]), lambda i: (i, 0)), vec, vec],
        out_specs=row,
        out_shape=jax.ShapeDtypeStruct((m, d), F32),
        compiler_params=_cparams(("parallel",)),
        name=name,
    )(x, ya, yb, gates, g.reshape(1, d), b.reshape(1, d))


def _wkv_body(r_ref, w_ref, k_ref, v_ref, a_ref, b_ref, s0_ref, y_ref, s_ref, *, tc, n):
    t_blk = pl.program_id(1)

    @pl.when(t_blk == 0)
    def _init():
        s_ref[...] = s0_ref[...]

    def row(ref, t, j):
        return ref[t, pl.ds(j, 1), :]

    def step(t, carry):
        sa = jnp.zeros((n, LANES), F32)
        for j in range(n):
            sa = sa + s_ref[j] * row(a_ref, t, j)
        v_t = v_ref[t]
        y = jnp.zeros((n, LANES), F32)
        for j in range(n):
            s_new = s_ref[j] * row(w_ref, t, j) + sa * row(b_ref, t, j) + v_t * row(k_ref, t, j)
            s_ref[j] = s_new
            y = y + s_new * row(r_ref, t, j)
        y_ref[t] = y
        return carry

    lax.fori_loop(0, tc, step, 0)


def _wkv_scan(r, w, k, v, a, b, s0, *, tc, name="wkv_scan"):
    t_len, n, lanes = r.shape
    assert lanes % LANES == 0 and t_len % tc == 0
    seq = pl.BlockSpec((tc, n, LANES), lambda l, t: (t, 0, l))
    st = pl.BlockSpec((n, n, LANES), lambda l, t: (0, 0, l))
    return pl.pallas_call(
        functools.partial(_wkv_body, tc=tc, n=n),
        grid=(lanes // LANES, t_len // tc),
        in_specs=[seq] * 6 + [st],
        out_specs=[seq, st],
        out_shape=[jax.ShapeDtypeStruct((t_len, n, lanes), F32),
                   jax.ShapeDtypeStruct((n, n, lanes), F32)],
        compiler_params=_cparams(("parallel", "arbitrary")),
        name=name,
    )(r, w, k, v, a, b, s0)


CHUNK = 128
SUB = 32


def _head_normalize(x_t, hl, n):
    x3 = x_t.reshape(hl, n, CHUNK)
    nrm = jnp.sqrt(jnp.sum(x3 * x3, axis=1, keepdims=True))
    return (x3 / jnp.maximum(nrm, 1e-12)).reshape(hl * n, CHUNK)


def _to_scan_body(*refs, mode, nb, hl, n):
    n_in = {"copy": 1, "k2": 2, "kk": 1, "kb": 2}[mode]
    tiles = [refs[i * nb:(i + 1) * nb] for i in range(n_in)]
    pos = n_in * nb
    par_ref = None
    if mode != "copy":
        par_ref = refs[pos]
        pos += 1
    o_ref, sc_ref = refs[pos], refs[pos + 1]

    for b in range(nb):
        if mode == "copy":
            res = tiles[0][b][...].T
        elif mode == "k2":
            k, a = tiles[0][b][...], tiles[1][b][...]
            res = (k * (1.0 + (a - 1.0) * par_ref[...])).T
        elif mode == "kk":
            res = _head_normalize((tiles[0][b][...] * par_ref[...]).T, hl, n)
        else:
            kk = _head_normalize((tiles[0][b][...] * par_ref[...]).T, hl, n)
            res = kk * tiles[1][b][...].T
        sc_ref[b] = res

    for j in range(n):
        slab = jnp.concatenate([sc_ref.at[b][pl.ds(j, hl, stride=n), :] for b in range(nb)], axis=0)
        slab_t = slab.T
        for s in range(CHUNK // SUB):
            o_ref[s, pl.ds(j * SUB, SUB), :] = slab_t[s * SUB:(s + 1) * SUB, :]


def _to_scan(inputs, par, *, mode, nb, seq, heads, n, name):
    hl = LANES // nb
    width = hl * n
    n_hh = heads // hl
    n_c = seq // CHUNK
    in_specs, args = [], []
    for x in inputs:
        for b in range(nb):
            in_specs.append(pl.BlockSpec((CHUNK, width), lambda hh, c, b=b: (b * n_c + c, hh)))
            args.append(x)
    if par is not None:
        in_specs.append(pl.BlockSpec((1, width), lambda hh, c: (0, hh)))
        args.append(par.reshape(1, heads * n))
    return pl.pallas_call(
        functools.partial(_to_scan_body, mode=mode, nb=nb, hl=hl, n=n),
        grid=(n_hh, n_c),
        in_specs=in_specs,
        out_specs=pl.BlockSpec((CHUNK // SUB, n * SUB, LANES), lambda hh, c: (c, 0, hh)),
        out_shape=jax.ShapeDtypeStruct((seq // SUB, n * SUB, n_hh * LANES), F32),
        scratch_shapes=[pltpu.VMEM((nb, width, CHUNK), F32)],
        compiler_params=_cparams(("parallel", "parallel")),
        name=name,
    )(*args)


def _wkv_sl_body(r_ref, w_ref, k_ref, v_ref, kk_ref, b_ref, y_ref, s_ref, *, n):
    @pl.when(pl.program_id(1) == 0)
    def _init():
        s_ref[...] = jnp.zeros_like(s_ref)

    def row(ref, t, j):
        return ref[pl.ds(j * SUB + t, 1), :]

    def step(t, skk):
        t_next = jnp.minimum(t + 1, SUB - 1)
        v_t = v_ref[pl.ds(t, n, stride=SUB), :]
        y = jnp.zeros((n, LANES), F32)
        skk_next = jnp.zeros((n, LANES), F32)
        for j in range(n):
            s_new = s_ref[j] * row(w_ref, t, j) - skk * row(b_ref, t, j) + v_t * row(k_ref, t, j)
            s_ref[j] = s_new
            y = y + s_new * row(r_ref, t, j)
            skk_next = skk_next + s_new * row(kk_ref, t_next, j)
        y_ref[pl.ds(t, n, stride=SUB), :] = y
        return skk_next

    skk0 = jnp.zeros((n, LANES), F32)
    for j in range(n):
        skk0 = skk0 + s_ref[j] * row(kk_ref, 0, j)
    lax.fori_loop(0, SUB, step, skk0)


def _wkv_scan_sl(r, w, k, v, kk, b, *, n, name="wkv_scan_prompt"):
    n_blk, rows, lanes = r.shape
    seq = pl.BlockSpec((None, rows, LANES), lambda l, t: (t, 0, l))
    st = pl.BlockSpec((n, n, LANES), lambda l, t: (0, 0, l))
    return pl.pallas_call(
        functools.partial(_wkv_sl_body, n=n),
        grid=(lanes // LANES, n_blk),
        in_specs=[seq] * 6,
        out_specs=[seq, st],
        out_shape=[jax.ShapeDtypeStruct((n_blk, rows, lanes), F32),
                   jax.ShapeDtypeStruct((n, n, lanes), F32)],
        compiler_params=_cparams(("parallel", "arbitrary")),
        name=name,
    )(r, w, k, v, kk, b)


def _from_scan_body(y_ref, r_ref, k_ref, v_ref, rk_ref, lg_ref, lb_ref, g_ref, o_ref, z_ref, sc_ref,
                    *, nb, hl, n, eps):
    b = pl.program_id(2)

    @pl.when(b == 0)
    def _fill():
        for s in range(CHUNK // SUB):
            def blk(ref, j, s=s):
                return ref[s, pl.ds(j * SUB, SUB), :]

            coef = jnp.zeros((SUB, LANES), F32)
            tot = jnp.zeros((SUB, LANES), F32)
            for j in range(n):
                coef = coef + blk(r_ref, j) * blk(k_ref, j) * rk_ref[pl.ds(j, 1), :]
                tot = tot + blk(y_ref, j)
            mu = tot / n
            var = jnp.zeros((SUB, LANES), F32)
            for j in range(n):
                d = blk(y_ref, j) - mu
                var = var + d * d
            inv = lax.rsqrt(var / n + eps)
            for j in range(n):
                z = (blk(y_ref, j) - mu) * inv * lg_ref[pl.ds(j, 1), :] + lb_ref[pl.ds(j, 1), :]
                z_ref[j, pl.ds(s * SUB, SUB), :] = z + coef * blk(v_ref, j)
        for j in range(n):
            slab_t = z_ref[j].T
            for bb in range(nb):
                sc_ref.at[bb][pl.ds(j, hl, stride=n), :] = slab_t[bb * hl:(bb + 1) * hl, :]

    o_ref[...] = (sc_ref[b].T * g_ref[...]).astype(o_ref.dtype)


def _from_scan(y, r, k2, v, rk_t, lg_t, lb_t, g, *, nb, seq, heads, n, eps, name="rwkv_post"):
    hl = LANES // nb
    width = hl * n
    n_hh = heads // hl
    n_c = seq // CHUNK
    sl = pl.BlockSpec((CHUNK // SUB, n * SUB, LANES), lambda hh, c, b: (c, 0, hh))
    par = pl.BlockSpec((n, LANES), lambda hh, c, b: (0, hh))
    tok = pl.BlockSpec((CHUNK, width), lambda hh, c, b: (b * n_c + c, hh))
    return pl.pallas_call(
        functools.partial(_from_scan_body, nb=nb, hl=hl, n=n, eps=eps),
        grid=(n_hh, n_c, nb),
        in_specs=[sl, sl, sl, sl, par, par, par, tok],
        out_specs=tok,
        out_shape=jax.ShapeDtypeStruct((nb * seq, heads * n), BF16),
        scratch_shapes=[pltpu.VMEM((n, CHUNK, LANES), F32), pltpu.VMEM((nb, width, CHUNK), F32)],
        compiler_params=_cparams(("parallel", "parallel", "arbitrary")),
        name=name,
    )(y, r, k2, v, rk_t, lg_t, lb_t, g)


def _head_param_lanes(p, nb, heads, n):
    hl = LANES // nb
    t = p.reshape(heads // hl, 1, hl, n)
    t = jnp.broadcast_to(t, (heads // hl, nb, hl, n))
    return jnp.transpose(t, (3, 0, 1, 2)).reshape(n, (heads // hl) * LANES).astype(F32)


SPAN = 128
ATTN_UNROLL = 5


def _attn_block(q, k, v, causal_own, scale):
    s = lax.dot_general(q.astype(BF16), k.astype(BF16), (((1,), (1,)), ((), ())),
                        preferred_element_type=F32) * scale
    qi = lax.broadcasted_iota(jnp.int32, s.shape, 0)
    ki = lax.broadcasted_iota(jnp.int32, s.shape, 1)
    if causal_own:
        mask = ki <= qi
    else:
        mask = (ki >= qi) & (ki <= qi + SPAN)
    s = jnp.where(mask, s, -jnp.inf)
    m = jnp.max(s, axis=1, keepdims=True)
    p = jnp.exp(s - m)
    l = jnp.sum(p, axis=1, keepdims=True)
    o = jnp.dot(p.astype(BF16), v.astype(BF16), preferred_element_type=F32)
    return o, m, l


def _dilated_attn_body(*refs, dils, seq, scale):
    ng = len(dils)
    q_refs, k_refs, v_refs = refs[:ng], refs[ng:2 * ng], refs[2 * ng:3 * ng]
    o_ref = refs[3 * ng]
    scratch = refs[3 * ng + 1:]
    acc_refs, m_refs, l_refs = scratch[:ng], scratch[ng:2 * ng], scratch[2 * ng:]

    for g, dil in enumerate(dils):
        q_ref, k_ref, v_ref = q_refs[g], k_refs[g], v_refs[g]
        acc_ref, m_ref, l_ref = acc_refs[g], m_refs[g], l_refs[g]
        nblk = seq // (dil * SPAN)

        def rows(start, size, dil=dil):
            if dil == 1:
                return pl.ds(start, size)
            return pl.ds(start, size, stride=dil)

        def put(sl, res, acc_ref=acc_ref, m_ref=m_ref, l_ref=l_ref):
            o, m, l = res
            acc_ref[sl, :] = o
            m_ref[sl, :] = m
            l_ref[sl, :] = l

        for r in range(dil):
            sl0 = rows(r, SPAN)
            put(sl0, _attn_block(q_ref[sl0, :], k_ref[sl0, :], v_ref[sl0, :], True, scale))

            def body(nb, carry, r=r, dil=dil, rows=rows, put=put, q_ref=q_ref, k_ref=k_ref, v_ref=v_ref):
                q_sl = rows(r + dil * SPAN * nb, SPAN)
                kv_sl = rows(r + dil * SPAN * (nb - 1), 2 * SPAN)
                put(q_sl, _attn_block(q_ref[q_sl, :], k_ref[kv_sl, :], v_ref[kv_sl, :], False, scale))
                return carry

            if nblk > 1:
                lax.fori_loop(1, nblk, body, 0, unroll=ATTN_UNROLL)

    m_all = m_refs[0][...]
    for g in range(1, ng):
        m_all = jnp.maximum(m_all, m_refs[g][...])
    num = jnp.zeros(acc_refs[0].shape, F32)
    den = jnp.zeros(m_all.shape, F32)
    for g in range(ng):
        e = jnp.exp(m_refs[g][...] - m_all)
        num = num + e * acc_refs[g][...]
        den = den + e * l_refs[g][...]
    o_ref[...] = (num / den).astype(o_ref.dtype)


def _dilated_attn(q_hm, k_hm, v_hm, *, batch, seq, heads, dils, v_head0=0, name="dilated_attn"):
    e = q_hm.shape[-1]
    ng = len(dils)

    def spec(g, h0=0):
        return pl.BlockSpec((None, seq, e), lambda b, h, g=g: (h0 + g * heads + h, b, 0))

    specs = [spec(g) for g in range(ng)]
    return pl.pallas_call(
        functools.partial(_dilated_attn_body, dils=dils, seq=seq, scale=e ** -0.5),
        grid=(batch, heads),
        in_specs=specs * 2 + [spec(g, v_head0) for g in range(ng)],
        out_specs=pl.BlockSpec((seq, e), lambda b, h: (b, h)),
        out_shape=jax.ShapeDtypeStruct((batch * seq, heads * e), BF16),
        scratch_shapes=([pltpu.VMEM((seq, e), F32)] * ng + [pltpu.VMEM((seq, 1), F32)] * (2 * ng)),
        compiler_params=_cparams(("parallel", "parallel")),
        name=name,
    )(*([q_hm] * ng + [k_hm] * ng + [v_hm] * ng))


def _sample_attn_body(q_ref, kn_ref, vn_ref, *rest, ng, heads, scale):
    cache_refs, o_ref = rest[:ng], rest[ng]
    for h in range(heads):
        parts = []
        for g in range(ng):
            row = g * heads + h
            q = q_ref[pl.ds(row, 1), :]
            kc = cache_refs[g][:, 0, h, :]
            vc = cache_refs[g][:, 1, h, :]
            kn, vn = kn_ref[pl.ds(row, 1), :], vn_ref[pl.ds(row, 1), :]
            qb = q.astype(BF16)
            s = lax.dot_general(qb, kc.astype(BF16), (((1,), (1,)), ((), ())),
                                preferred_element_type=F32) * scale
            s_n = jnp.sum(qb.astype(F32) * kn.astype(BF16).astype(F32), axis=1, keepdims=True) * scale
            m = jnp.maximum(jnp.max(s, axis=1, keepdims=True), s_n)
            p, p_n = jnp.exp(s - m), jnp.exp(s_n - m)
            l = jnp.sum(p, axis=1, keepdims=True) + p_n
            o = jnp.dot(p.astype(BF16), vc.astype(BF16), preferred_element_type=F32)
            o = o + p_n.astype(BF16).astype(F32) * vn.astype(BF16).astype(F32)
            parts.append((o, m, l))
        m_all = parts[0][1]
        for _, m, _ in parts[1:]:
            m_all = jnp.maximum(m_all, m)
        num = jnp.zeros_like(parts[0][0])
        den = jnp.zeros_like(m_all)
        for o, m, l in parts:
            e = jnp.exp(m - m_all)
            num = num + e * o
            den = den + e * l
        o_ref[pl.ds(h, 1), :] = (num / den).astype(o_ref.dtype)


def _sample_attn(q_s, k_new, v_new, caches, dils, *, heads, name="sample_attn"):
    bs, _, e = q_s.shape
    ng = len(dils)
    tok = pl.BlockSpec((None, ng * heads, e), lambda b: (b, 0, 0))
    in_specs, args = [tok, tok, tok], [q_s, k_new, v_new]
    for c, dil in zip(caches, dils):
        w = c.shape[1]
        assert w == SPAN * dil
        args.append(c.reshape(bs, SPAN, dil, 2, heads, e))
        in_specs.append(pl.BlockSpec((None, SPAN, None, 2, heads, e), lambda b: (b, 0, 0, 0, 0, 0)))
    return pl.pallas_call(
        functools.partial(_sample_attn_body, ng=ng, heads=heads, scale=e ** -0.5),
        grid=(bs,),
        in_specs=in_specs,
        out_specs=pl.BlockSpec((None, heads, e), lambda b: (b, 0, 0)),
        out_shape=jax.ShapeDtypeStruct((bs, heads, e), BF16),
        compiler_params=_cparams(("parallel",)),
        name=name,
    )(*args)


def _router_body(x_ref, w_ref, o_ref):
    o_ref[...] = jnp.dot(x_ref[...], w_ref[...], precision=lax.Precision.HIGHEST,
                         preferred_element_type=F32)


def _router_logits(x, router, *, tm, name="router_logits"):
    m, d = x.shape
    n_exp = router.shape[1]
    w = jnp.zeros((d, LANES), F32).at[:, :n_exp].set(router.astype(F32))
    out = pl.pallas_call(
        _router_body,
        grid=(m // tm,),
        in_specs=[pl.BlockSpec((tm, d), lambda i: (i, 0)), pl.BlockSpec((d, LANES), lambda i: (0, 0))],
        out_specs=pl.BlockSpec((tm, LANES), lambda i: (i, 0)),
        out_shape=jax.ShapeDtypeStruct((m, LANES), F32),
        compiler_params=_cparams(("parallel",)),
        name=name,
    )(x, w)
    return out[:, :n_exp]


def _moe_in_body(be_ref, ok_ref, src_ref, x_ref, wg_ref, wu_ref, o_ref, wgb_ref, wub_ref):
    r = pl.program_id(1)
    changed = (r == 0) | (be_ref[r] != be_ref[jnp.maximum(r - 1, 0)])

    @pl.when(changed)
    def _cast():
        wgb_ref[...] = wg_ref[...].astype(BF16)
        wub_ref[...] = wu_ref[...].astype(BF16)

    @pl.when(ok_ref[r] != 0)
    def _compute():
        x = x_ref[...]
        g = jnp.dot(x, wgb_ref[...], preferred_element_type=F32)
        u = jnp.dot(x, wub_ref[...], preferred_element_type=F32)
        o_ref[...] = (g * jax.nn.sigmoid(g) * u).astype(o_ref.dtype)

    @pl.when(ok_ref[r] == 0)
    def _skip():
        o_ref[...] = jnp.zeros_like(o_ref)


def _moe_in(x_buf, w_in, blk_e, blk_ok, blk_src, *, tn, name="moe_swiglu_in"):
    rows, d = x_buf.shape
    n_exp, _, n2 = w_in.shape
    hid = n2 // 2
    nt = hid // tn
    rb = rows // MOE_ROWS
    grid_spec = pltpu.PrefetchScalarGridSpec(
        num_scalar_prefetch=3,
        grid=(nt, rb),
        in_specs=[pl.BlockSpec((MOE_ROWS, d), lambda j, r, be, ok, src: (src[r], 0)),
                  pl.BlockSpec((None, d, tn), lambda j, r, be, ok, src: (be[r], 0, j)),
                  pl.BlockSpec((None, d, tn), lambda j, r, be, ok, src: (be[r], 0, j + nt))],
        out_specs=pl.BlockSpec((MOE_ROWS, tn), lambda j, r, be, ok, src: (r, j)),
        scratch_shapes=[pltpu.VMEM((d, tn), BF16), pltpu.VMEM((d, tn), BF16)],
    )
    return pl.pallas_call(
        _moe_in_body,
        grid_spec=grid_spec,
        out_shape=jax.ShapeDtypeStruct((rows, hid), BF16),
        compiler_params=_cparams(("arbitrary", "arbitrary")),
        name=name,
    )(blk_e, blk_ok, blk_src, x_buf, w_in, w_in)


def _moe_out_body(pe_ref, nv_ref, src_ref, h_ref, w_ref, o_ref, acc_ref, *, nk):
    p = pl.program_id(0)
    k = pl.program_id(2)
    nv = nv_ref[p]

    @pl.when(k == 0)
    def _init():
        acc_ref[...] = jnp.zeros_like(acc_ref)

    @pl.when(nv == 2)
    def _both():
        acc_ref[...] += jnp.dot(h_ref[...], w_ref[...].astype(BF16), preferred_element_type=F32)

    @pl.when(nv == 1)
    def _first():
        acc_ref[:MOE_ROWS, :] += jnp.dot(h_ref[:MOE_ROWS, :], w_ref[...].astype(BF16),
                                         preferred_element_type=F32)

    @pl.when(k == nk - 1)
    def _fin():
        o_ref[...] = acc_ref[...]


def _moe_out(h_buf, w_out, pair_e, pair_nv, pair_src, *, tn, tk, name="moe_out"):
    rows, hid = h_buf.shape
    n_exp, _, d = w_out.shape
    n_pair = rows // (2 * MOE_ROWS)
    nk = hid // tk

    def k_eff(p, k, nv):
        return jnp.where(nv[p] > 0, k, nk - 1)

    grid_spec = pltpu.PrefetchScalarGridSpec(
        num_scalar_prefetch=3,
        grid=(n_pair, d // tn, nk),
        in_specs=[pl.BlockSpec((2 * MOE_ROWS, tk), lambda p, j, k, pe, nv, src: (src[p], k_eff(p, k, nv))),
                  pl.BlockSpec((None, tk, tn), lambda p, j, k, pe, nv, src: (pe[p], k_eff(p, k, nv), j))],
        out_specs=pl.BlockSpec((2 * MOE_ROWS, tn), lambda p, j, k, pe, nv, src: (p, j)),
        scratch_shapes=[pltpu.VMEM((2 * MOE_ROWS, tn), F32)],
    )
    return pl.pallas_call(
        functools.partial(_moe_out_body, nk=nk),
        grid_spec=grid_spec,
        out_shape=jax.ShapeDtypeStruct((rows, d), F32),
        compiler_params=_cparams(("arbitrary", "arbitrary", "arbitrary")),
        name=name,
    )(pair_e, pair_nv, pair_src, h_buf, w_out)


def _moe(x_f32, n_tok, router, w_in, w_out):
    rows, d = x_f32.shape
    n_exp = router.shape[1]
    top_k = 2
    logits = _router_logits(x_f32, router, tm=520)[:n_tok]
    top_v, top_i = lax.top_k(logits, top_k)
    gates = jax.nn.softmax(top_v, axis=-1)
    n_asg = n_tok * top_k
    e = top_i.reshape(n_asg).astype(jnp.int32)
    tok = jnp.repeat(jnp.arange(n_tok, dtype=jnp.int32), top_k)
    order = jnp.argsort(e)
    e_s, tok_s = e[order], tok[order]
    counts = jnp.bincount(e, length=n_exp).astype(jnp.int32)
    starts = jnp.cumsum(counts) - counts
    padded = -(-counts // MOE_ROWS) * MOE_ROWS
    region = -(-padded // (2 * MOE_ROWS)) * (2 * MOE_ROWS)
    region_ends = jnp.cumsum(region)
    region_starts = region_ends - region
    dest = (region_starts[e_s] + jnp.arange(n_asg, dtype=jnp.int32) - starts[e_s]).astype(jnp.int32)
    n_pair = (-(-n_asg // MOE_ROWS) + 2 * n_exp + 1) // 2
    n_blk = 2 * n_pair
    src_tok = (jnp.arange(n_blk * MOE_ROWS, dtype=jnp.int32) % n_tok).at[dest].set(tok_s)
    pos = jnp.zeros((n_asg,), jnp.int32).at[order].set(dest).reshape(n_tok, top_k)
    blk_start = jnp.arange(n_blk, dtype=jnp.int32) * MOE_ROWS
    blk_e = jnp.minimum(jnp.searchsorted(region_ends, blk_start, side='right'), n_exp - 1).astype(jnp.int32)
    blk_ok = ((blk_start - region_starts[blk_e]) < padded[blk_e]).astype(jnp.int32)
    blk_src = lax.cummax(jnp.where(blk_ok > 0, jnp.arange(n_blk, dtype=jnp.int32), 0))
    pair_nv = blk_ok.reshape(n_pair, 2).sum(axis=1).astype(jnp.int32)
    pair_src = lax.cummax(jnp.where(pair_nv > 0, jnp.arange(n_pair, dtype=jnp.int32), 0))

    blk_ids = jnp.arange(n_blk, dtype=jnp.int32)
    blk_next = lax.cummin(jnp.where(blk_ok > 0, blk_ids, n_blk - 1), axis=0, reverse=True)
    x_buf = x_f32[src_tok].astype(BF16)
    h_buf = _moe_in(x_buf, w_in, blk_e[blk_next], blk_ok, blk_src, tn=512)
    y_buf = _moe_out(h_buf, w_out, blk_e[::2], pair_nv, pair_src, tn=min(2048, d), tk=1024)
    pos = jnp.zeros((rows, top_k), jnp.int32).at[:n_tok].set(pos)
    gates = jnp.zeros((rows, top_k), F32).at[:n_tok].set(gates)
    return y_buf[pos[:, 0]], y_buf[pos[:, 1]], gates


def _rope_tables(pos, hd):
    inv = ROPE_THETA ** (-jnp.arange(0, hd, 2, dtype=F32) / hd)
    ang = pos.astype(F32)[:, None] * inv[None, :]
    ang = jnp.concatenate([ang, ang], -1)
    return jnp.cos(ang), jnp.sin(ang)


def _softplus(z):
    return jnp.maximum(z, 0.0) + jnp.log(1.0 + jnp.exp(-jnp.abs(z)))


def _decay_epilogue(y):
    return jnp.exp(-jnp.exp(-_softplus(-y) - 0.5))


def kernel(x_prompt, x_sample, state_wkv, state_shift, cache_kv_w128, cache_kv_w512, cache_kv_w2048,
           ln_g, ln_b, rwkv_mix, rwkv_w_rkv, rwkv_w0, rwkv_w1, rwkv_w2, rwkv_a0, rwkv_a1, rwkv_a2,
           rwkv_g1, rwkv_g2, rwkv_k_k, rwkv_k_a, rwkv_r_k, rwkv_lnx_g, rwkv_lnx_b, rwkv_w_out,
           attn_w_kv, attn_w_q, attn_w_out, ffn_w_in, ffn_w_out, moe_router, moe_w_in, moe_w_out):
    bp, seq, d = x_prompt.shape
    bs, seq_s, _ = x_sample.shape
    depth = ln_g.shape[0]
    assert depth == 2 and seq_s == 1
    heads_r, hd_r = rwkv_r_k.shape[1], rwkv_r_k.shape[2]
    past_len = PAST_LEN
    groups = ((128, 1), (512, 4), (2048, 16))
    ng = len(groups)
    hd_a = 128
    heads_a = attn_w_out.shape[1] // hd_a
    alpha = (2.0 * depth) ** 0.25
    lnx_eps = 1e-5 * hd_r

    n_p = bp * seq
    n_tok = n_p + bs
    tm = 2080
    rows = -(-n_tok // tm) * tm

    def pad_rows(a):
        return jnp.concatenate([a, jnp.zeros((rows - a.shape[0],) + a.shape[1:], a.dtype)], 0)

    x0 = pad_rows(jnp.concatenate([x_prompt.reshape(n_p, d), x_sample.reshape(bs, d)], 0))
    prev_p = jnp.concatenate([jnp.zeros((bp, 1, d), F32), x_prompt[:, :-1]], 1).reshape(n_p, d)
    x_prev = pad_rows(jnp.concatenate([prev_p, state_shift[0]], 0))
    xx = x_prev - x0
    xs = [(x0 + xx * rwkv_mix[0, c]).astype(BF16) for c in range(6)]

    w_rkv = rwkv_w_rkv.reshape(3, d, d)
    mm = functools.partial(_matmul, tm=tm)
    r = mm(xs[0], w_rkv, 0, tn=1024, tk=1024, name="rwkv_r")
    k = mm(xs[1], w_rkv, 1, tn=1024, tk=1024, name="rwkv_k")
    v = mm(xs[2], w_rkv, 2, tn=1024, tk=1024, name="rwkv_v")
    lw, la, lg = rwkv_w1.shape[2], rwkv_a1.shape[2], rwkv_g1.shape[2]
    hw = mm(xs[3], rwkv_w1, 0, tn=lw, tk=1024, out_dtype=BF16, epilogue=jnp.tanh, name="rwkv_w1")
    decay = mm(hw, rwkv_w2, 0, tn=1024, tk=lw, bias=rwkv_w0[0], epilogue=_decay_epilogue, name="rwkv_w2")
    ha = mm(xs[4], rwkv_a1, 0, tn=la, tk=1024, out_dtype=BF16, name="rwkv_a1")
    a = mm(ha, rwkv_a2, 0, tn=1024, tk=la, bias=rwkv_a0[0], epilogue=jax.nn.sigmoid, name="rwkv_a2")
    hg = mm(xs[5], rwkv_g1, 0, tn=lg, tk=1024, out_dtype=BF16, epilogue=jax.nn.sigmoid, name="rwkv_g1")
    g = mm(hg, rwkv_g2, 0, tn=1024, tk=lg, name="rwkv_g2")

    hl = LANES // bp
    assert LANES % bp == 0 and heads_r % hl == 0 and seq % CHUNK == 0
    kw = dict(nb=bp, seq=seq, heads=heads_r, n=hd_r)
    r_sl = _to_scan([r], None, mode="copy", name="sl_r", **kw)
    w_sl = _to_scan([decay], None, mode="copy", name="sl_w", **kw)
    v_sl = _to_scan([v], None, mode="copy", name="sl_v", **kw)
    k_sl = _to_scan([k, a], rwkv_k_a[0], mode="k2", name="sl_k", **kw)
    kk_sl = _to_scan([k], rwkv_k_k[0], mode="kk", name="sl_kk", **kw)
    b_sl = _to_scan([k, a], rwkv_k_k[0], mode="kb", name="sl_b", **kw)
    y_sl, s_p = _wkv_scan_sl(r_sl, w_sl, k_sl, v_sl, kk_sl, b_sl, n=hd_r)
    tab = functools.partial(_head_param_lanes, nb=bp, heads=heads_r, n=hd_r)
    y_p = _from_scan(y_sl, r_sl, k_sl, v_sl, tab(rwkv_r_k.reshape(-1)), tab(rwkv_lnx_g[0]),
                     tab(rwkv_lnx_b[0]), g, eps=lnx_eps, **kw)
    s_p = s_p.reshape(hd_r, hd_r, heads_r // hl, bp, hl)
    prompt_wkv = jnp.transpose(s_p, (3, 2, 4, 1, 0)).reshape(1, bp, heads_r, hd_r, hd_r).astype(state_wkv.dtype)

    def hs(t):
        return t[n_p:n_tok].reshape(bs, heads_r, hd_r)

    kk_s = hs(k) * rwkv_k_k[0].reshape(heads_r, hd_r)
    kk_s = kk_s / jnp.maximum(jnp.linalg.norm(kk_s, axis=-1, keepdims=True), 1e-12)
    k2_s = hs(k) * (1.0 + (hs(a) - 1.0) * rwkv_k_a[0].reshape(heads_r, hd_r))
    bonus_s = jnp.sum(hs(r) * k2_s * rwkv_r_k[0], -1, keepdims=True) * hs(v)

    def lanes_s(t):
        return jnp.transpose(t, (2, 0, 1)).reshape(1, hd_r, bs * heads_r)

    scan_s = (hs(r), hs(decay), k2_s, hs(v), -kk_s, kk_s * hs(a))
    s0_s = jnp.transpose(state_wkv[0].astype(F32), (3, 2, 0, 1)).reshape(hd_r, hd_r, bs * heads_r)
    y_s, s_s = _wkv_scan(*[lanes_s(t) for t in scan_s], s0_s, tc=1, name="wkv_scan_sample")
    y_s = jnp.transpose(y_s.reshape(hd_r, bs, heads_r), (1, 2, 0))
    sample_wkv = jnp.transpose(s_s.reshape(hd_r, hd_r, bs, heads_r), (2, 3, 1, 0))[None].astype(state_wkv.dtype)
    mu = jnp.mean(y_s, -1, keepdims=True)
    var = jnp.mean(jnp.square(y_s - mu), -1, keepdims=True)
    y_s = ((y_s - mu) * lax.rsqrt(var + lnx_eps)).reshape(bs, d) * rwkv_lnx_g[0] + rwkv_lnx_b[0]
    y_s = ((y_s + bonus_s.reshape(bs, d)) * g[n_p:n_tok]).astype(BF16)

    y = pad_rows(jnp.concatenate([y_p, y_s], 0))
    mix0 = mm(y, rwkv_w_out, 0, tn=1024, tk=1024, name="rwkv_out")
    x1, x1b = _res_ln(x0, mix0, ln_g[0, 0], ln_b[0, 0], alpha=alpha, tm=208, name="ln_0a")

    h = _swiglu_in(x1b, ffn_w_in, 0, tm=tm, tn=256, tk=2048, name="ffn_in")
    f0 = mm(h, ffn_w_out, 0, tn=1024, tk=1024, name="ffn_out")
    x2, x2b = _res_ln(x1, f0, ln_g[0, 1], ln_b[0, 1], alpha=alpha, tm=208, name="ln_0b")

    pos = jnp.concatenate([jnp.tile(jnp.arange(seq, dtype=jnp.int32), bp),
                           jnp.full((bs,), past_len, jnp.int32),
                           jnp.zeros((rows - n_tok,), jnp.int32)])
    cos, sin = _rope_tables(pos, hd_a)
    half_sign = jnp.concatenate([-jnp.ones((hd_a // 2,), F32), jnp.ones((hd_a // 2,), F32)])
    sin_s = sin * half_sign
    n_qh = ng * heads_a
    tn_h = 1024
    kv_hm = _matmul_heads(x2b, attn_w_kv.reshape(1, d, -1), 0, cos, sin_s, n_rope=n_qh * hd_a // tn_h,
                          tm=tm, tn=tn_h, tk=1024, hd=hd_a, name="attn_kv")
    q_hm = _matmul_heads(x2b, attn_w_q, 0, cos, sin_s, n_rope=n_qh * hd_a // tn_h,
                         tm=tm, tn=tn_h, tk=1024, hd=hd_a, name="attn_q")
    dils = tuple(dl for _, dl in groups)
    att_p = _dilated_attn(q_hm, kv_hm, kv_hm, batch=bp, seq=seq, heads=heads_a, dils=dils, v_head0=n_qh)

    caches = (cache_kv_w128, cache_kv_w512, cache_kv_w2048)
    q_s = jnp.transpose(q_hm[:, n_p:n_tok], (1, 0, 2))
    kv_s = jnp.transpose(kv_hm[:, n_p:n_tok], (1, 0, 2)).reshape(bs, 2, n_qh, hd_a)
    att_s = _sample_attn(q_s, kv_s[:, 0], kv_s[:, 1], caches, dils, heads=heads_a)
    att = pad_rows(jnp.concatenate([att_p, att_s.reshape(bs, heads_a * hd_a)], 0))
    mix1 = mm(att, attn_w_out, 0, tn=1024, tk=1024, name="attn_out")
    x3, x3b = _res_ln(x2, mix1, ln_g[1, 0], ln_b[1, 0], alpha=alpha, tm=208, name="ln_1a")

    n_exp = moe_router.shape[-1]
    ya, yb, gates = _moe(x3, n_tok, moe_router.reshape(d, n_exp), moe_w_in.reshape(n_exp, d, -1),
                         moe_w_out.reshape(n_exp, -1, d))
    x4 = _combine_ln(x3, ya, yb, gates, ln_g[1, 1], ln_b[1, 1], alpha=alpha, tm=208)

    y_prompt = x4[:n_p].reshape(bp, seq, d)
    y_sample = x4[n_p:n_tok].reshape(bs, 1, d)
    prompt_shift = x_prompt[:, -1][None]
    sample_shift = x_sample[:, -1][None]
    prompt_kv, sample_kv = [], []
    kv_n = kv_s.reshape(bs, 1, 2, ng, heads_a, hd_a)
    for gi, (win, dil) in enumerate(groups):
        keep = min(win, seq)
        rows_g = _kv_window(kv_hm, group=gi, n_groups=ng, heads=heads_a, batch=bp, seq=seq, keep=keep,
                            name="kv_window_%d" % win)
        prompt_kv.append(rows_g.reshape(bp, keep, 2, heads_a, hd_a))
        sample_kv.append(kv_n[:, :, :, gi])
    return (y_prompt, y_sample, prompt_wkv, prompt_shift, prompt_kv[0], prompt_kv[1], prompt_kv[2],
            sample_wkv, sample_shift, sample_kv[0], sample_kv[1], sample_kv[2])
```

```python
import functools

import jax
import jax.numpy as jnp
from jax import lax
from jax.experimental import pallas as pl
from jax.experimental.pallas import tpu as pltpu

F32 = jnp.float32
BF16 = jnp.bfloat16

LANES = 128
SUBLANES = 8
VMEM_LIMIT = 56 * 1024 * 1024

LN_EPS = 1e-5
ROPE_THETA = 10000.0
PAST_LEN = 16384
MOE_ROWS = 512


def _cparams(sem):
    return pltpu.CompilerParams(dimension_semantics=sem, vmem_limit_bytes=VMEM_LIMIT)


def _mm_body(x_ref, w_ref, *rest, nk, k_rem, epilogue, has_bias):
    if has_bias:
        b_ref, o_ref, acc_ref = rest
    else:
        o_ref, acc_ref = rest
    k = pl.program_id(2)

    def prod(x, w):
        return jnp.dot(x.astype(BF16), w.astype(BF16), preferred_element_type=F32)

    def last_prod():
        x, w = x_ref[...], w_ref[...]
        if k_rem:
            col = lax.broadcasted_iota(jnp.int32, x.shape, 1)
            row = lax.broadcasted_iota(jnp.int32, w.shape, 0)
            x = jnp.where(col < k_rem, x, jnp.zeros_like(x))
            w = jnp.where(row < k_rem, w, jnp.zeros_like(w))
        return prod(x, w)

    def finish(y):
        if has_bias:
            y = y + b_ref[...]
        if epilogue is not None:
            y = epilogue(y)
        o_ref[...] = y.astype(o_ref.dtype)

    if nk == 1:
        finish(last_prod())
        return

    @pl.when(k == 0)
    def _first():
        acc_ref[...] = prod(x_ref[...], w_ref[...])

    if nk > 2:
        @pl.when((k > 0) & (k < nk - 1))
        def _middle():
            acc_ref[...] += prod(x_ref[...], w_ref[...])

    @pl.when(k == nk - 1)
    def _last():
        finish(acc_ref[...] + last_prod())


def _matmul(x, w3, li, *, tm, tn, tk, out_dtype=F32, bias=None, epilogue=None, name="mm"):
    m, kdim = x.shape
    _, kw, n = w3.shape
    assert kw == kdim and m % tm == 0 and n % tn == 0
    nk = pl.cdiv(kdim, tk)
    in_specs = [pl.BlockSpec((tm, tk), lambda i, j, k: (i, k)),
                pl.BlockSpec((None, tk, tn), lambda i, j, k: (li, k, j))]
    args = [x, w3]
    if bias is not None:
        in_specs.append(pl.BlockSpec((1, tn), lambda i, j, k: (0, j)))
        args.append(bias.reshape(1, n).astype(F32))
    return pl.pallas_call(
        functools.partial(_mm_body, nk=nk, k_rem=kdim % tk, epilogue=epilogue, has_bias=bias is not None),
        grid=(m // tm, n // tn, nk),
        in_specs=in_specs,
        out_specs=pl.BlockSpec((tm, tn), lambda i, j, k: (i, j)),
        out_shape=jax.ShapeDtypeStruct((m, n), out_dtype),
        scratch_shapes=[pltpu.VMEM((tm, tn), F32)],
        compiler_params=_cparams(("parallel", "parallel", "arbitrary")),
        name=name,
    )(*args)


def _mm_heads_body(x_ref, w_ref, cos_ref, sin_ref, o_ref, acc_ref, *, nk, n_rope, hd):
    j = pl.program_id(1)
    k = pl.program_id(2)
    assert nk >= 2

    def prod():
        return jnp.dot(x_ref[...].astype(BF16), w_ref[...].astype(BF16), preferred_element_type=F32)

    @pl.when(k == 0)
    def _first():
        acc_ref[...] = prod()

    if nk > 2:
        @pl.when((k > 0) & (k < nk - 1))
        def _middle():
            acc_ref[...] += prod()

    n_heads = acc_ref.shape[1] // hd

    @pl.when((k == nk - 1) & (j < n_rope))
    def _fin_rope():
        cos, sin = cos_ref[...], sin_ref[...]
        total = acc_ref[...] + prod()
        for h in range(n_heads):
            y = total[:, h * hd:(h + 1) * hd]
            o_ref[h] = y * cos + pltpu.roll(y, hd // 2, 1) * sin

    @pl.when((k == nk - 1) & (j >= n_rope))
    def _fin_plain():
        total = acc_ref[...] + prod()
        for h in range(n_heads):
            o_ref[h] = total[:, h * hd:(h + 1) * hd]


def _matmul_heads(x, w3, li, cos, sin_s, *, n_rope, tm, tn, tk, hd, name):
    m, kdim = x.shape
    _, _, n = w3.shape
    assert m % tm == 0 and n % tn == 0 and kdim % tk == 0 and tn % hd == 0
    nk = kdim // tk
    hpt = tn // hd
    return pl.pallas_call(
        functools.partial(_mm_heads_body, nk=nk, n_rope=n_rope, hd=hd),
        grid=(m // tm, n // tn, nk),
        in_specs=[pl.BlockSpec((tm, tk), lambda i, j, k: (i, k)),
                  pl.BlockSpec((None, tk, tn), lambda i, j, k: (li, k, j)),
                  pl.BlockSpec((tm, hd), lambda i, j, k: (i, 0)),
                  pl.BlockSpec((tm, hd), lambda i, j, k: (i, 0))],
        out_specs=pl.BlockSpec((hpt, tm, hd), lambda i, j, k: (j, i, 0)),
        out_shape=jax.ShapeDtypeStruct((n // hd, m, hd), F32),
        scratch_shapes=[pltpu.VMEM((tm, tn), F32)],
        compiler_params=_cparams(("parallel", "parallel", "arbitrary")),
        name=name,
    )(x, w3, cos, sin_s)


def _window_body(x_ref, o_ref, *, heads, hd):
    for h in range(heads):
        o_ref[:, h * hd:(h + 1) * hd] = x_ref[h]


def _kv_window(kv_hm, *, group, n_groups, heads, batch, seq, keep, name):
    hd = kv_hm.shape[-1]
    tmc = min(keep, 1024)
    assert keep % tmc == 0 and (seq - keep) % tmc == 0
    per_b = keep // tmc

    def in_map(i, c):
        b, t = i // per_b, i % per_b
        return (c * n_groups + group, (b * seq + seq - keep) // tmc + t, 0)

    return pl.pallas_call(
        functools.partial(_window_body, heads=heads, hd=hd),
        grid=(batch * per_b, 2),
        in_specs=[pl.BlockSpec((heads, tmc, hd), in_map)],
        out_specs=pl.BlockSpec((tmc, heads * hd), lambda i, c: (i, c)),
        out_shape=jax.ShapeDtypeStruct((batch * keep, 2 * heads * hd), F32),
        compiler_params=_cparams(("parallel", "parallel")),
        name=name,
    )(kv_hm)


def _swiglu_body(x_ref, wg_ref, wu_ref, o_ref, accg_ref, accu_ref, *, nk):
    k = pl.program_id(2)
    assert nk >= 2

    def prods():
        xb = x_ref[...].astype(BF16)
        return (jnp.dot(xb, wg_ref[...].astype(BF16), preferred_element_type=F32),
                jnp.dot(xb, wu_ref[...].astype(BF16), preferred_element_type=F32))

    @pl.when(k == 0)
    def _first():
        accg_ref[...], accu_ref[...] = prods()

    if nk > 2:
        @pl.when((k > 0) & (k < nk - 1))
        def _middle():
            pg, pu = prods()
            accg_ref[...] += pg
            accu_ref[...] += pu

    @pl.when(k == nk - 1)
    def _last():
        pg, pu = prods()
        g = accg_ref[...] + pg
        o_ref[...] = (g * jax.nn.sigmoid(g) * (accu_ref[...] + pu)).astype(o_ref.dtype)


def _swiglu_in(x, w3, li, *, tm, tn, tk, name="swiglu_in"):
    m, kdim = x.shape
    _, _, n2 = w3.shape
    hid = n2 // 2
    assert m % tm == 0 and hid % tn == 0 and kdim % tk == 0
    nk = kdim // tk
    nt = hid // tn
    return pl.pallas_call(
        functools.partial(_swiglu_body, nk=nk),
        grid=(m // tm, nt, nk),
        in_specs=[pl.BlockSpec((tm, tk), lambda i, j, k: (i, k)),
                  pl.BlockSpec((None, tk, tn), lambda i, j, k: (li, k, j)),
                  pl.BlockSpec((None, tk, tn), lambda i, j, k: (li, k, j + nt))],
        out_specs=pl.BlockSpec((tm, tn), lambda i, j, k: (i, j)),
        out_shape=jax.ShapeDtypeStruct((m, hid), BF16),
        scratch_shapes=[pltpu.VMEM((tm, tn), F32), pltpu.VMEM((tm, tn), F32)],
        compiler_params=_cparams(("parallel", "parallel", "arbitrary")),
        name=name,
    )(x, w3, w3)


def _ln_body(x_ref, f_ref, g_ref, b_ref, o_ref, ob_ref, *, alpha):
    z = alpha * x_ref[...] + f_ref[...]
    mu = jnp.mean(z, -1, keepdims=True)
    zc = z - mu
    var = jnp.mean(zc * zc, -1, keepdims=True)
    y = zc * lax.rsqrt(var + LN_EPS) * g_ref[...] + b_ref[...]
    o_ref[...] = y
    ob_ref[...] = y.astype(BF16)


def _res_ln(x, f, g, b, *, alpha, tm, name="res_ln"):
    m, d = x.shape
    assert m % tm == 0
    row = pl.BlockSpec((tm, d), lambda i: (i, 0))
    vec = pl.BlockSpec((1, d), lambda i: (0, 0))
    return pl.pallas_call(
        functools.partial(_ln_body, alpha=alpha),
        grid=(m // tm,),
        in_specs=[row, row, vec, vec],
        out_specs=[row, row],
        out_shape=[jax.ShapeDtypeStruct((m, d), F32), jax.ShapeDtypeStruct((m, d), BF16)],
        compiler_params=_cparams(("parallel",)),
        name=name,
    )(x, f, g.reshape(1, d), b.reshape(1, d))


def _combine_ln_body(x_ref, ya_ref, yb_ref, gt_ref, g_ref, b_ref, o_ref, *, alpha):
    gt = gt_ref[...]
    f = ya_ref[...] * gt[:, 0:1] + yb_ref[...] * gt[:, 1:2]
    z = alpha * x_ref[...] + f
    mu = jnp.mean(z, -1, keepdims=True)
    zc = z - mu
    var = jnp.mean(zc * zc, -1, keepdims=True)
    o_ref[...] = zc * lax.rsqrt(var + LN_EPS) * g_ref[...] + b_ref[...]


def _combine_ln(x, ya, yb, gates, g, b, *, alpha, tm, name="moe_combine_ln"):
    m, d = x.shape
    assert m % tm == 0
    row = pl.BlockSpec((tm, d), lambda i: (i, 0))
    vec = pl.BlockSpec((1, d), lambda i: (0, 0))
    return pl.pallas_call(
        functools.partial(_combine_ln_body, alpha=alpha),
        grid=(m // tm,),
        in_specs=[row, row, row, pl.BlockSpec((tm, gates.shape[1]), lambda i: (i, 0)), vec, vec],
        out_specs=row,
        out_shape=jax.ShapeDtypeStruct((m, d), F32),
        compiler_params=_cparams(("parallel",)),
        name=name,
    )(x, ya, yb, gates, g.reshape(1, d), b.reshape(1, d))


def _wkv_body(r_ref, w_ref, k_ref, v_ref, a_ref, b_ref, s0_ref, y_ref, s_ref, *, tc, n):
    t_blk = pl.program_id(1)

    @pl.when(t_blk == 0)
    def _init():
        s_ref[...] = s0_ref[...]

    def row(ref, t, j):
        return ref[t, pl.ds(j, 1), :]

    def step(t, carry):
        sa = jnp.zeros((n, LANES), F32)
        for j in range(n):
            sa = sa + s_ref[j] * row(a_ref, t, j)
        v_t = v_ref[t]
        y = jnp.zeros((n, LANES), F32)
        for j in range(n):
            s_new = s_ref[j] * row(w_ref, t, j) + sa * row(b_ref, t, j) + v_t * row(k_ref, t, j)
            s_ref[j] = s_new
            y = y + s_new * row(r_ref, t, j)
        y_ref[t] = y
        return carry

    lax.fori_loop(0, tc, step, 0)


def _wkv_scan(r, w, k, v, a, b, s0, *, tc, name="wkv_scan"):
    t_len, n, lanes = r.shape
    assert lanes % LANES == 0 and t_len % tc == 0
    seq = pl.BlockSpec((tc, n, LANES), lambda l, t: (t, 0, l))
    st = pl.BlockSpec((n, n, LANES), lambda l, t: (0, 0, l))
    return pl.pallas_call(
        functools.partial(_wkv_body, tc=tc, n=n),
        grid=(lanes // LANES, t_len // tc),
        in_specs=[seq] * 6 + [st],
        out_specs=[seq, st],
        out_shape=[jax.ShapeDtypeStruct((t_len, n, lanes), F32),
                   jax.ShapeDtypeStruct((n, n, lanes), F32)],
        compiler_params=_cparams(("parallel", "arbitrary")),
        name=name,
    )(r, w, k, v, a, b, s0)


CHUNK = 128
SUB = 32


def _head_normalize(x_t, hl, n):
    x3 = x_t.reshape(hl, n, CHUNK)
    nrm = jnp.sqrt(jnp.sum(x3 * x3, axis=1, keepdims=True))
    return (x3 / jnp.maximum(nrm, 1e-12)).reshape(hl * n, CHUNK)


def _to_scan_body(*refs, mode, nb, hl, n):
    n_in = {"copy": 1, "k2": 2, "kk": 1, "kb": 2}[mode]
    tiles = [refs[i * nb:(i + 1) * nb] for i in range(n_in)]
    pos = n_in * nb
    par_ref = None
    if mode != "copy":
        par_ref = refs[pos]
        pos += 1
    o_ref, sc_ref = refs[pos], refs[pos + 1]

    for b in range(nb):
        if mode == "copy":
            res = tiles[0][b][...].T
        elif mode == "k2":
            k, a = tiles[0][b][...], tiles[1][b][...]
            res = (k * (1.0 + (a - 1.0) * par_ref[...])).T
        elif mode == "kk":
            res = _head_normalize((tiles[0][b][...] * par_ref[...]).T, hl, n)
        else:
            kk = _head_normalize((tiles[0][b][...] * par_ref[...]).T, hl, n)
            res = kk * tiles[1][b][...].T
        sc_ref[b] = res

    for j in range(n):
        slab = jnp.concatenate([sc_ref.at[b][pl.ds(j, hl, stride=n), :] for b in range(nb)], axis=0)
        slab_t = slab.T
        for s in range(CHUNK // SUB):
            o_ref[s, pl.ds(j * SUB, SUB), :] = slab_t[s * SUB:(s + 1) * SUB, :]


def _to_scan(inputs, par, *, mode, nb, seq, heads, n, name):
    hl = LANES // nb
    width = hl * n
    n_hh = heads // hl
    n_c = seq // CHUNK
    in_specs, args = [], []
    for x in inputs:
        for b in range(nb):
            in_specs.append(pl.BlockSpec((CHUNK, width), lambda hh, c, b=b: (b * n_c + c, hh)))
            args.append(x)
    if par is not None:
        in_specs.append(pl.BlockSpec((1, width), lambda hh, c: (0, hh)))
        args.append(par.reshape(1, heads * n))
    return pl.pallas_call(
        functools.partial(_to_scan_body, mode=mode, nb=nb, hl=hl, n=n),
        grid=(n_hh, n_c),
        in_specs=in_specs,
        out_specs=pl.BlockSpec((CHUNK // SUB, n * SUB, LANES), lambda hh, c: (c, 0, hh)),
        out_shape=jax.ShapeDtypeStruct((seq // SUB, n * SUB, n_hh * LANES), F32),
        scratch_shapes=[pltpu.VMEM((nb, width, CHUNK), F32)],
        compiler_params=_cparams(("parallel", "parallel")),
        name=name,
    )(*args)


def _wkv_sl_body(r_ref, w_ref, k_ref, v_ref, kk_ref, b_ref, y_ref, s_ref, *, n):
    @pl.when(pl.program_id(1) == 0)
    def _init():
        s_ref[...] = jnp.zeros_like(s_ref)

    def row(ref, t, j):
        return ref[pl.ds(j * SUB + t, 1), :]

    def step(t, skk):
        t_next = jnp.minimum(t + 1, SUB - 1)
        v_t = v_ref[pl.ds(t, n, stride=SUB), :]
        y = jnp.zeros((n, LANES), F32)
        skk_next = jnp.zeros((n, LANES), F32)
        for j in range(n):
            s_new = s_ref[j] * row(w_ref, t, j) - skk * row(b_ref, t, j) + v_t * row(k_ref, t, j)
            s_ref[j] = s_new
            y = y + s_new * row(r_ref, t, j)
            skk_next = skk_next + s_new * row(kk_ref, t_next, j)
        y_ref[pl.ds(t, n, stride=SUB), :] = y
        return skk_next

    skk0 = jnp.zeros((n, LANES), F32)
    for j in range(n):
        skk0 = skk0 + s_ref[j] * row(kk_ref, 0, j)
    lax.fori_loop(0, SUB, step, skk0)


def _wkv_scan_sl(r, w, k, v, kk, b, *, n, name="wkv_scan_prompt"):
    n_blk, rows, lanes = r.shape
    seq = pl.BlockSpec((None, rows, LANES), lambda l, t: (t, 0, l))
    st = pl.BlockSpec((n, n, LANES), lambda l, t: (0, 0, l))
    return pl.pallas_call(
        functools.partial(_wkv_sl_body, n=n),
        grid=(lanes // LANES, n_blk),
        in_specs=[seq] * 6,
        out_specs=[seq, st],
        out_shape=[jax.ShapeDtypeStruct((n_blk, rows, lanes), F32),
                   jax.ShapeDtypeStruct((n, n, lanes), F32)],
        compiler_params=_cparams(("parallel", "arbitrary")),
        name=name,
    )(r, w, k, v, kk, b)


def _from_scan_body(y_ref, r_ref, k_ref, v_ref, rk_ref, lg_ref, lb_ref, g_ref, o_ref, z_ref, sc_ref,
                    *, nb, hl, n, eps):
    b = pl.program_id(2)

    @pl.when(b == 0)
    def _fill():
        for s in range(CHUNK // SUB):
            def blk(ref, j, s=s):
                return ref[s, pl.ds(j * SUB, SUB), :]

            coef = jnp.zeros((SUB, LANES), F32)
            tot = jnp.zeros((SUB, LANES), F32)
            for j in range(n):
                coef = coef + blk(r_ref, j) * blk(k_ref, j) * rk_ref[pl.ds(j, 1), :]
                tot = tot + blk(y_ref, j)
            mu = tot / n
            var = jnp.zeros((SUB, LANES), F32)
            for j in range(n):
                d = blk(y_ref, j) - mu
                var = var + d * d
            inv = lax.rsqrt(var / n + eps)
            for j in range(n):
                z = (blk(y_ref, j) - mu) * inv * lg_ref[pl.ds(j, 1), :] + lb_ref[pl.ds(j, 1), :]
                z_ref[j, pl.ds(s * SUB, SUB), :] = z + coef * blk(v_ref, j)
        for j in range(n):
            slab_t = z_ref[j].T
            for bb in range(nb):
                sc_ref.at[bb][pl.ds(j, hl, stride=n), :] = slab_t[bb * hl:(bb + 1) * hl, :]

    o_ref[...] = (sc_ref[b].T * g_ref[...]).astype(o_ref.dtype)


def _from_scan(y, r, k2, v, rk_t, lg_t, lb_t, g, *, nb, seq, heads, n, eps, name="rwkv_post"):
    hl = LANES // nb
    width = hl * n
    n_hh = heads // hl
    n_c = seq // CHUNK
    sl = pl.BlockSpec((CHUNK // SUB, n * SUB, LANES), lambda hh, c, b: (c, 0, hh))
    par = pl.BlockSpec((n, LANES), lambda hh, c, b: (0, hh))
    tok = pl.BlockSpec((CHUNK, width), lambda hh, c, b: (b * n_c + c, hh))
    return pl.pallas_call(
        functools.partial(_from_scan_body, nb=nb, hl=hl, n=n, eps=eps),
        grid=(n_hh, n_c, nb),
        in_specs=[sl, sl, sl, sl, par, par, par, tok],
        out_specs=tok,
        out_shape=jax.ShapeDtypeStruct((nb * seq, heads * n), BF16),
        scratch_shapes=[pltpu.VMEM((n, CHUNK, LANES), F32), pltpu.VMEM((nb, width, CHUNK), F32)],
        compiler_params=_cparams(("parallel", "parallel", "arbitrary")),
        name=name,
    )(y, r, k2, v, rk_t, lg_t, lb_t, g)


def _head_param_lanes(p, nb, heads, n):
    hl = LANES // nb
    t = p.reshape(heads // hl, 1, hl, n)
    t = jnp.broadcast_to(t, (heads // hl, nb, hl, n))
    return jnp.transpose(t, (3, 0, 1, 2)).reshape(n, (heads // hl) * LANES).astype(F32)


SPAN = 128
ATTN_UNROLL = 5


def _attn_block(q, k, v, causal_own, scale):
    s = lax.dot_general(q.astype(BF16), k.astype(BF16), (((1,), (1,)), ((), ())),
                        preferred_element_type=F32) * scale
    qi = lax.broadcasted_iota(jnp.int32, s.shape, 0)
    ki = lax.broadcasted_iota(jnp.int32, s.shape, 1)
    if causal_own:
        mask = ki <= qi
    else:
        mask = (ki >= qi) & (ki <= qi + SPAN)
    s = jnp.where(mask, s, -jnp.inf)
    m = jnp.max(s, axis=1, keepdims=True)
    p = jnp.exp(s - m)
    l = jnp.sum(p, axis=1, keepdims=True)
    o = jnp.dot(p.astype(BF16), v.astype(BF16), preferred_element_type=F32)
    return o, m, l


def _dilated_attn_body(*refs, dils, seq, scale):
    ng = len(dils)
    q_refs, k_refs, v_refs = refs[:ng], refs[ng:2 * ng], refs[2 * ng:3 * ng]
    o_ref = refs[3 * ng]
    scratch = refs[3 * ng + 1:]
    acc_refs, m_refs, l_refs = scratch[:ng], scratch[ng:2 * ng], scratch[2 * ng:]

    for g, dil in enumerate(dils):
        q_ref, k_ref, v_ref = q_refs[g], k_refs[g], v_refs[g]
        acc_ref, m_ref, l_ref = acc_refs[g], m_refs[g], l_refs[g]
        nblk = seq // (dil * SPAN)

        def rows(start, size, dil=dil):
            if dil == 1:
                return pl.ds(start, size)
            return pl.ds(start, size, stride=dil)

        def put(sl, res, acc_ref=acc_ref, m_ref=m_ref, l_ref=l_ref):
            o, m, l = res
            acc_ref[sl, :] = o
            m_ref[sl, :] = m
            l_ref[sl, :] = l

        for r in range(dil):
            sl0 = rows(r, SPAN)
            put(sl0, _attn_block(q_ref[sl0, :], k_ref[sl0, :], v_ref[sl0, :], True, scale))

            def body(nb, carry, r=r, dil=dil, rows=rows, put=put, q_ref=q_ref, k_ref=k_ref, v_ref=v_ref):
                q_sl = rows(r + dil * SPAN * nb, SPAN)
                kv_sl = rows(r + dil * SPAN * (nb - 1), 2 * SPAN)
                put(q_sl, _attn_block(q_ref[q_sl, :], k_ref[kv_sl, :], v_ref[kv_sl, :], False, scale))
                return carry

            if nblk > 1:
                lax.fori_loop(1, nblk, body, 0, unroll=ATTN_UNROLL)

    m_all = m_refs[0][...]
    for g in range(1, ng):
        m_all = jnp.maximum(m_all, m_refs[g][...])
    num = jnp.zeros(acc_refs[0].shape, F32)
    den = jnp.zeros(m_all.shape, F32)
    for g in range(ng):
        e = jnp.exp(m_refs[g][...] - m_all)
        num = num + e * acc_refs[g][...]
        den = den + e * l_refs[g][...]
    o_ref[...] = (num / den).astype(o_ref.dtype)


def _dilated_attn(q_hm, k_hm, v_hm, *, batch, seq, heads, dils, v_head0=0, name="dilated_attn"):
    e = q_hm.shape[-1]
    ng = len(dils)

    def spec(g, h0=0):
        return pl.BlockSpec((None, seq, e), lambda b, h, g=g: (h0 + g * heads + h, b, 0))

    specs = [spec(g) for g in range(ng)]
    return pl.pallas_call(
        functools.partial(_dilated_attn_body, dils=dils, seq=seq, scale=e ** -0.5),
        grid=(batch, heads),
        in_specs=specs * 2 + [spec(g, v_head0) for g in range(ng)],
        out_specs=pl.BlockSpec((seq, e), lambda b, h: (b, h)),
        out_shape=jax.ShapeDtypeStruct((batch * seq, heads * e), BF16),
        scratch_shapes=([pltpu.VMEM((seq, e), F32)] * ng + [pltpu.VMEM((seq, 1), F32)] * (2 * ng)),
        compiler_params=_cparams(("parallel", "parallel")),
        name=name,
    )(*([q_hm] * ng + [k_hm] * ng + [v_hm] * ng))


def _sample_attn_body(q_ref, kn_ref, vn_ref, *rest, ng, heads, scale):
    cache_refs, o_ref = rest[:ng], rest[ng]
    for h in range(heads):
        parts = []
        for g in range(ng):
            row = g * heads + h
            q = q_ref[pl.ds(row, 1), :]
            kc = cache_refs[g][:, 0, h, :]
            vc = cache_refs[g][:, 1, h, :]
            kn, vn = kn_ref[pl.ds(row, 1), :], vn_ref[pl.ds(row, 1), :]
            qb = q.astype(BF16)
            s = lax.dot_general(qb, kc.astype(BF16), (((1,), (1,)), ((), ())),
                                preferred_element_type=F32) * scale
            s_n = jnp.sum(qb.astype(F32) * kn.astype(BF16).astype(F32), axis=1, keepdims=True) * scale
            m = jnp.maximum(jnp.max(s, axis=1, keepdims=True), s_n)
            p, p_n = jnp.exp(s - m), jnp.exp(s_n - m)
            l = jnp.sum(p, axis=1, keepdims=True) + p_n
            o = jnp.dot(p.astype(BF16), vc.astype(BF16), preferred_element_type=F32)
            o = o + p_n.astype(BF16).astype(F32) * vn.astype(BF16).astype(F32)
            parts.append((o, m, l))
        m_all = parts[0][1]
        for _, m, _ in parts[1:]:
            m_all = jnp.maximum(m_all, m)
        num = jnp.zeros_like(parts[0][0])
        den = jnp.zeros_like(m_all)
        for o, m, l in parts:
            e = jnp.exp(m - m_all)
            num = num + e * o
            den = den + e * l
        o_ref[pl.ds(h, 1), :] = (num / den).astype(o_ref.dtype)


def _sample_attn(q_s, k_new, v_new, caches, dils, *, heads, name="sample_attn"):
    bs, _, e = q_s.shape
    ng = len(dils)
    tok = pl.BlockSpec((None, ng * heads, e), lambda b: (b, 0, 0))
    in_specs, args = [tok, tok, tok], [q_s, k_new, v_new]
    for c, dil in zip(caches, dils):
        w = c.shape[1]
        assert w == SPAN * dil
        args.append(c.reshape(bs, SPAN, dil, 2, heads, e))
        in_specs.append(pl.BlockSpec((None, SPAN, None, 2, heads, e), lambda b: (b, 0, 0, 0, 0, 0)))
    return pl.pallas_call(
        functools.partial(_sample_attn_body, ng=ng, heads=heads, scale=e ** -0.5),
        grid=(bs,),
        in_specs=in_specs,
        out_specs=pl.BlockSpec((None, heads, e), lambda b: (b, 0, 0)),
        out_shape=jax.ShapeDtypeStruct((bs, heads, e), BF16),
        compiler_params=_cparams(("parallel",)),
        name=name,
    )(*args)


def _router_body(x_ref, w_ref, o_ref):
    o_ref[...] = jnp.dot(x_ref[...], w_ref[...], precision=lax.Precision.HIGHEST,
                         preferred_element_type=F32)


def _router_logits(x, router, *, tm, name="router_logits"):
    m, d = x.shape
    n_exp = router.shape[1]
    w = jnp.zeros((d, LANES), F32).at[:, :n_exp].set(router.astype(F32))
    out = pl.pallas_call(
        _router_body,
        grid=(m // tm,),
        in_specs=[pl.BlockSpec((tm, d), lambda i: (i, 0)), pl.BlockSpec((d, LANES), lambda i: (0, 0))],
        out_specs=pl.BlockSpec((tm, LANES), lambda i: (i, 0)),
        out_shape=jax.ShapeDtypeStruct((m, LANES), F32),
        compiler_params=_cparams(("parallel",)),
        name=name,
    )(x, w)
    return out[:, :n_exp]


def _moe_in_body(be_ref, ok_ref, src_ref, x_ref, wg_ref, wu_ref, o_ref, wgb_ref, wub_ref):
    r = pl.program_id(1)
    changed = (r == 0) | (be_ref[r] != be_ref[jnp.maximum(r - 1, 0)])

    @pl.when(changed)
    def _cast():
        wgb_ref[...] = wg_ref[...].astype(BF16)
        wub_ref[...] = wu_ref[...].astype(BF16)

    @pl.when(ok_ref[r] != 0)
    def _compute():
        x = x_ref[...]
        g = jnp.dot(x, wgb_ref[...], preferred_element_type=F32)
        u = jnp.dot(x, wub_ref[...], preferred_element_type=F32)
        o_ref[...] = (g * jax.nn.sigmoid(g) * u).astype(o_ref.dtype)

    @pl.when(ok_ref[r] == 0)
    def _skip():
        o_ref[...] = jnp.zeros_like(o_ref)


def _moe_in(x_buf, w_in, blk_e, blk_ok, blk_src, *, tn, name="moe_swiglu_in"):
    rows, d = x_buf.shape
    n_exp, _, n2 = w_in.shape
    hid = n2 // 2
    nt = hid // tn
    rb = rows // MOE_ROWS
    grid_spec = pltpu.PrefetchScalarGridSpec(
        num_scalar_prefetch=3,
        grid=(nt, rb),
        in_specs=[pl.BlockSpec((MOE_ROWS, d), lambda j, r, be, ok, src: (src[r], 0)),
                  pl.BlockSpec((None, d, tn), lambda j, r, be, ok, src: (be[r], 0, j)),
                  pl.BlockSpec((None, d, tn), lambda j, r, be, ok, src: (be[r], 0, j + nt))],
        out_specs=pl.BlockSpec((MOE_ROWS, tn), lambda j, r, be, ok, src: (r, j)),
        scratch_shapes=[pltpu.VMEM((d, tn), BF16), pltpu.VMEM((d, tn), BF16)],
    )
    return pl.pallas_call(
        _moe_in_body,
        grid_spec=grid_spec,
        out_shape=jax.ShapeDtypeStruct((rows, hid), BF16),
        compiler_params=_cparams(("arbitrary", "arbitrary")),
        name=name,
    )(blk_e, blk_ok, blk_src, x_buf, w_in, w_in)


def _moe_out_body(pe_ref, nv_ref, src_ref, h_ref, w_ref, o_ref, acc_ref, *, nk):
    p = pl.program_id(0)
    k = pl.program_id(2)
    nv = nv_ref[p]

    @pl.when(k == 0)
    def _init():
        acc_ref[...] = jnp.zeros_like(acc_ref)

    @pl.when(nv == 2)
    def _both():
        acc_ref[...] += jnp.dot(h_ref[...], w_ref[...].astype(BF16), preferred_element_type=F32)

    @pl.when(nv == 1)
    def _first():
        acc_ref[:MOE_ROWS, :] += jnp.dot(h_ref[:MOE_ROWS, :], w_ref[...].astype(BF16),
                                         preferred_element_type=F32)

    @pl.when(k == nk - 1)
    def _fin():
        o_ref[...] = acc_ref[...]


def _moe_out(h_buf, w_out, pair_e, pair_nv, pair_src, *, tn, tk, name="moe_out"):
    rows, hid = h_buf.shape
    n_exp, _, d = w_out.shape
    n_pair = rows // (2 * MOE_ROWS)
    nk = hid // tk

    def k_eff(p, k, nv):
        return jnp.where(nv[p] > 0, k, nk - 1)

    grid_spec = pltpu.PrefetchScalarGridSpec(
        num_scalar_prefetch=3,
        grid=(n_pair, d // tn, nk),
        in_specs=[pl.BlockSpec((2 * MOE_ROWS, tk), lambda p, j, k, pe, nv, src: (src[p], k_eff(p, k, nv))),
                  pl.BlockSpec((None, tk, tn), lambda p, j, k, pe, nv, src: (pe[p], k_eff(p, k, nv), j))],
        out_specs=pl.BlockSpec((2 * MOE_ROWS, tn), lambda p, j, k, pe, nv, src: (p, j)),
        scratch_shapes=[pltpu.VMEM((2 * MOE_ROWS, tn), F32)],
    )
    return pl.pallas_call(
        functools.partial(_moe_out_body, nk=nk),
        grid_spec=grid_spec,
        out_shape=jax.ShapeDtypeStruct((rows, d), F32),
        compiler_params=_cparams(("arbitrary", "arbitrary", "arbitrary")),
        name=name,
    )(pair_e, pair_nv, pair_src, h_buf, w_out)


def _moe(x_f32, n_tok, router, w_in, w_out):
    rows, d = x_f32.shape
    n_exp = router.shape[1]
    top_k = 2
    logits = _router_logits(x_f32, router, tm=520)[:n_tok]
    top_v, top_i = lax.top_k(logits, top_k)
    gates = jax.nn.softmax(top_v, axis=-1)
    n_asg = n_tok * top_k
    e = top_i.reshape(n_asg).astype(jnp.int32)
    tok = jnp.repeat(jnp.arange(n_tok, dtype=jnp.int32), top_k)
    order = jnp.argsort(e)
    e_s, tok_s = e[order], tok[order]
    counts = jnp.bincount(e, length=n_exp).astype(jnp.int32)
    starts = jnp.cumsum(counts) - counts
    padded = -(-counts // MOE_ROWS) * MOE_ROWS
    region = -(-padded // (2 * MOE_ROWS)) * (2 * MOE_ROWS)
    region_ends = jnp.cumsum(region)
    region_starts = region_ends - region
    dest = (region_starts[e_s] + jnp.arange(n_asg, dtype=jnp.int32) - starts[e_s]).astype(jnp.int32)
    n_pair = (-(-n_asg // MOE_ROWS) + 2 * n_exp + 1) // 2
    n_blk = 2 * n_pair
    src_tok = (jnp.arange(n_blk * MOE_ROWS, dtype=jnp.int32) % n_tok).at[dest].set(tok_s)
    pos = jnp.zeros((n_asg,), jnp.int32).at[order].set(dest).reshape(n_tok, top_k)
    blk_start = jnp.arange(n_blk, dtype=jnp.int32) * MOE_ROWS
    blk_e = jnp.minimum(jnp.searchsorted(region_ends, blk_start, side='right'), n_exp - 1).astype(jnp.int32)
    blk_ok = ((blk_start - region_starts[blk_e]) < padded[blk_e]).astype(jnp.int32)
    blk_src = lax.cummax(jnp.where(blk_ok > 0, jnp.arange(n_blk, dtype=jnp.int32), 0))
    pair_nv = blk_ok.reshape(n_pair, 2).sum(axis=1).astype(jnp.int32)
    pair_src = lax.cummax(jnp.where(pair_nv > 0, jnp.arange(n_pair, dtype=jnp.int32), 0))

    blk_ids = jnp.arange(n_blk, dtype=jnp.int32)
    blk_next = lax.cummin(jnp.where(blk_ok > 0, blk_ids, n_blk - 1), axis=0, reverse=True)
    x_buf = x_f32[src_tok].astype(BF16)
    h_buf = _moe_in(x_buf, w_in, blk_e[blk_next], blk_ok, blk_src, tn=512)
    y_buf = _moe_out(h_buf, w_out, blk_e[::2], pair_nv, pair_src, tn=min(2048, d), tk=1024)
    pos = jnp.zeros((rows, top_k), jnp.int32).at[:n_tok].set(pos)
    gates = jnp.zeros((rows, top_k), F32).at[:n_tok].set(gates)
    return y_buf[pos[:, 0]], y_buf[pos[:, 1]], gates


def _rope_tables(pos, hd):
    inv = ROPE_THETA ** (-jnp.arange(0, hd, 2, dtype=F32) / hd)
    ang = pos.astype(F32)[:, None] * inv[None, :]
    ang = jnp.concatenate([ang, ang], -1)
    return jnp.cos(ang), jnp.sin(ang)


def _softplus(z):
    return jnp.maximum(z, 0.0) + jnp.log(1.0 + jnp.exp(-jnp.abs(z)))


def _decay_epilogue(y):
    return jnp.exp(-jnp.exp(-_softplus(-y) - 0.5))


def kernel(x_prompt, x_sample, state_wkv, state_shift, cache_kv_w128, cache_kv_w512, cache_kv_w2048,
           ln_g, ln_b, rwkv_mix, rwkv_w_rkv, rwkv_w0, rwkv_w1, rwkv_w2, rwkv_a0, rwkv_a1, rwkv_a2,
           rwkv_g1, rwkv_g2, rwkv_k_k, rwkv_k_a, rwkv_r_k, rwkv_lnx_g, rwkv_lnx_b, rwkv_w_out,
           attn_w_kv, attn_w_q, attn_w_out, ffn_w_in, ffn_w_out, moe_router, moe_w_in, moe_w_out):
    bp, seq, d = x_prompt.shape
    bs, seq_s, _ = x_sample.shape
    depth = ln_g.shape[0]
    assert depth == 2 and seq_s == 1
    heads_r, hd_r = rwkv_r_k.shape[1], rwkv_r_k.shape[2]
    past_len = PAST_LEN
    groups = ((128, 1), (512, 4), (2048, 16))
    ng = len(groups)
    hd_a = 128
    heads_a = attn_w_out.shape[1] // hd_a
    alpha = (2.0 * depth) ** 0.25
    lnx_eps = 1e-5 * hd_r

    n_p = bp * seq
    n_tok = n_p + bs
    tm = 2080
    rows = -(-n_tok // tm) * tm

    def pad_rows(a):
        return jnp.concatenate([a, jnp.zeros((rows - a.shape[0],) + a.shape[1:], a.dtype)], 0)

    x0 = pad_rows(jnp.concatenate([x_prompt.reshape(n_p, d), x_sample.reshape(bs, d)], 0))
    prev_p = jnp.concatenate([jnp.zeros((bp, 1, d), F32), x_prompt[:, :-1]], 1).reshape(n_p, d)
    x_prev = pad_rows(jnp.concatenate([prev_p, state_shift[0]], 0))
    xx = x_prev - x0
    xs = [(x0 + xx * rwkv_mix[0, c]).astype(BF16) for c in range(6)]

    w_rkv = rwkv_w_rkv.reshape(3, d, d)
    mm = functools.partial(_matmul, tm=tm)
    r = mm(xs[0], w_rkv, 0, tn=1024, tk=1024, name="rwkv_r")
    k = mm(xs[1], w_rkv, 1, tn=1024, tk=1024, name="rwkv_k")
    v = mm(xs[2], w_rkv, 2, tn=1024, tk=1024, name="rwkv_v")
    lw, la, lg = rwkv_w1.shape[2], rwkv_a1.shape[2], rwkv_g1.shape[2]
    hw = mm(xs[3], rwkv_w1, 0, tn=lw, tk=1024, out_dtype=BF16, epilogue=jnp.tanh, name="rwkv_w1")
    decay = mm(hw, rwkv_w2, 0, tn=1024, tk=lw, bias=rwkv_w0[0], epilogue=_decay_epilogue, name="rwkv_w2")
    ha = mm(xs[4], rwkv_a1, 0, tn=la, tk=1024, out_dtype=BF16, name="rwkv_a1")
    a = mm(ha, rwkv_a2, 0, tn=1024, tk=la, bias=rwkv_a0[0], epilogue=jax.nn.sigmoid, name="rwkv_a2")
    hg = mm(xs[5], rwkv_g1, 0, tn=lg, tk=1024, out_dtype=BF16, epilogue=jax.nn.sigmoid, name="rwkv_g1")
    g = mm(hg, rwkv_g2, 0, tn=1024, tk=lg, name="rwkv_g2")

    hl = LANES // bp
    assert LANES % bp == 0 and heads_r % hl == 0 and seq % CHUNK == 0
    kw = dict(nb=bp, seq=seq, heads=heads_r, n=hd_r)
    r_sl = _to_scan([r], None, mode="copy", name="sl_r", **kw)
    w_sl = _to_scan([decay], None, mode="copy", name="sl_w", **kw)
    v_sl = _to_scan([v], None, mode="copy", name="sl_v", **kw)
    k_sl = _to_scan([k, a], rwkv_k_a[0], mode="k2", name="sl_k", **kw)
    kk_sl = _to_scan([k], rwkv_k_k[0], mode="kk", name="sl_kk", **kw)
    b_sl = _to_scan([k, a], rwkv_k_k[0], mode="kb", name="sl_b", **kw)
    y_sl, s_p = _wkv_scan_sl(r_sl, w_sl, k_sl, v_sl, kk_sl, b_sl, n=hd_r)
    tab = functools.partial(_head_param_lanes, nb=bp, heads=heads_r, n=hd_r)
    y_p = _from_scan(y_sl, r_sl, k_sl, v_sl, tab(rwkv_r_k.reshape(-1)), tab(rwkv_lnx_g[0]),
                     tab(rwkv_lnx_b[0]), g, eps=lnx_eps, **kw)
    s_p = s_p.reshape(hd_r, hd_r, heads_r // hl, bp, hl)
    prompt_wkv = jnp.transpose(s_p, (3, 2, 4, 1, 0)).reshape(1, bp, heads_r, hd_r, hd_r).astype(state_wkv.dtype)

    def hs(t):
        return t[n_p:n_tok].reshape(bs, heads_r, hd_r)

    kk_s = hs(k) * rwkv_k_k[0].reshape(heads_r, hd_r)
    kk_s = kk_s / jnp.maximum(jnp.linalg.norm(kk_s, axis=-1, keepdims=True), 1e-12)
    k2_s = hs(k) * (1.0 + (hs(a) - 1.0) * rwkv_k_a[0].reshape(heads_r, hd_r))
    bonus_s = jnp.sum(hs(r) * k2_s * rwkv_r_k[0], -1, keepdims=True) * hs(v)

    def lanes_s(t):
        return jnp.transpose(t, (2, 0, 1)).reshape(1, hd_r, bs * heads_r)

    scan_s = (hs(r), hs(decay), k2_s, hs(v), -kk_s, kk_s * hs(a))
    s0_s = jnp.transpose(state_wkv[0].astype(F32), (3, 2, 0, 1)).reshape(hd_r, hd_r, bs * heads_r)
    y_s, s_s = _wkv_scan(*[lanes_s(t) for t in scan_s], s0_s, tc=1, name="wkv_scan_sample")
    y_s = jnp.transpose(y_s.reshape(hd_r, bs, heads_r), (1, 2, 0))
    sample_wkv = jnp.transpose(s_s.reshape(hd_r, hd_r, bs, heads_r), (2, 3, 1, 0))[None].astype(state_wkv.dtype)
    mu = jnp.mean(y_s, -1, keepdims=True)
    var = jnp.mean(jnp.square(y_s - mu), -1, keepdims=True)
    y_s = ((y_s - mu) * lax.rsqrt(var + lnx_eps)).reshape(bs, d) * rwkv_lnx_g[0] + rwkv_lnx_b[0]
    y_s = ((y_s + bonus_s.reshape(bs, d)) * g[n_p:n_tok]).astype(BF16)

    y = pad_rows(jnp.concatenate([y_p, y_s], 0))
    mix0 = mm(y, rwkv_w_out, 0, tn=1024, tk=1024, name="rwkv_out")
    x1, x1b = _res_ln(x0, mix0, ln_g[0, 0], ln_b[0, 0], alpha=alpha, tm=208, name="ln_0a")

    h = _swiglu_in(x1b, ffn_w_in, 0, tm=tm, tn=256, tk=2048, name="ffn_in")
    f0 = mm(h, ffn_w_out, 0, tn=1024, tk=1024, name="ffn_out")
    x2, x2b = _res_ln(x1, f0, ln_g[0, 1], ln_b[0, 1], alpha=alpha, tm=208, name="ln_0b")

    pos = jnp.concatenate([jnp.tile(jnp.arange(seq, dtype=jnp.int32), bp),
                           jnp.full((bs,), past_len, jnp.int32),
                           jnp.zeros((rows - n_tok,), jnp.int32)])
    cos, sin = _rope_tables(pos, hd_a)
    half_sign = jnp.concatenate([-jnp.ones((hd_a // 2,), F32), jnp.ones((hd_a // 2,), F32)])
    sin_s = sin * half_sign
    n_qh = ng * heads_a
    tn_h = 1024
    kv_hm = _matmul_heads(x2b, attn_w_kv.reshape(1, d, -1), 0, cos, sin_s, n_rope=n_qh * hd_a // tn_h,
                          tm=tm, tn=tn_h, tk=1024, hd=hd_a, name="attn_kv")
    q_hm = _matmul_heads(x2b, attn_w_q, 0, cos, sin_s, n_rope=n_qh * hd_a // tn_h,
                         tm=tm, tn=tn_h, tk=1024, hd=hd_a, name="attn_q")
    dils = tuple(dl for _, dl in groups)
    att_p = _dilated_attn(q_hm, kv_hm, kv_hm, batch=bp, seq=seq, heads=heads_a, dils=dils, v_head0=n_qh)

    caches = (cache_kv_w128, cache_kv_w512, cache_kv_w2048)
    q_s = jnp.transpose(q_hm[:, n_p:n_tok], (1, 0, 2))
    kv_s = jnp.transpose(kv_hm[:, n_p:n_tok], (1, 0, 2)).reshape(bs, 2, n_qh, hd_a)
    att_s = _sample_attn(q_s, kv_s[:, 0], kv_s[:, 1], caches, dils, heads=heads_a)
    att = pad_rows(jnp.concatenate([att_p, att_s.reshape(bs, heads_a * hd_a)], 0))
    mix1 = mm(att, attn_w_out, 0, tn=1024, tk=1024, name="attn_out")
    x3, x3b = _res_ln(x2, mix1, ln_g[1, 0], ln_b[1, 0], alpha=alpha, tm=208, name="ln_1a")

    n_exp = moe_router.shape[-1]
    ya, yb, gates = _moe(x3, n_tok, moe_router.reshape(d, n_exp), moe_w_in.reshape(n_exp, d, -1),
                         moe_w_out.reshape(n_exp, -1, d))
    x4 = _combine_ln(x3, ya, yb, gates, ln_g[1, 1], ln_b[1, 1], alpha=alpha, tm=208)

    y_prompt = x4[:n_p].reshape(bp, seq, d)
    y_sample = x4[n_p:n_tok].reshape(bs, 1, d)
    prompt_shift = x_prompt[:, -1][None]
    sample_shift = x_sample[:, -1][None]
    prompt_kv, sample_kv = [], []
    kv_n = kv_s.reshape(bs, 1, 2, ng, heads_a, hd_a)
    for gi, (win, dil) in enumerate(groups):
        keep = min(win, seq)
        rows_g = _kv_window(kv_hm, group=gi, n_groups=ng, heads=heads_a, batch=bp, seq=seq, keep=keep,
                            name="kv_window_%d" % win)
        prompt_kv.append(rows_g.reshape(bp, keep, 2, heads_a, hd_a))
        sample_kv.append(kv_n[:, :, :, gi])
    return (y_prompt, y_sample, prompt_wkv, prompt_shift, prompt_kv[0], prompt_kv[1], prompt_kv[2],
            sample_wkv, sample_shift, sample_kv[0], sample_kv[1], sample_kv[2])
```

```python
import functools

import jax
import jax.numpy as jnp
from jax import lax
from jax.experimental import pallas as pl
from jax.experimental.pallas import tpu as pltpu

F32 = jnp.float32
BF16 = jnp.bfloat16

LANES = 128
SUBLANES = 8
VMEM_LIMIT = 56 * 1024 * 1024

LN_EPS = 1e-5
ROPE_THETA = 10000.0
PAST_LEN = 16384
MOE_ROWS = 512


def _cparams(sem):
    return pltpu.CompilerParams(dimension_semantics=sem, vmem_limit_bytes=VMEM_LIMIT)


def _mm_body(x_ref, w_ref, *rest, nk, k_rem, epilogue, has_bias):
    if has_bias:
        b_ref, o_ref, acc_ref = rest
    else:
        o_ref, acc_ref = rest
    k = pl.program_id(2)

    def prod(x, w):
        return jnp.dot(x.astype(BF16), w.astype(BF16), preferred_element_type=F32)

    def last_prod():
        x, w = x_ref[...], w_ref[...]
        if k_rem:
            col = lax.broadcasted_iota(jnp.int32, x.shape, 1)
            row = lax.broadcasted_iota(jnp.int32, w.shape, 0)
            x = jnp.where(col < k_rem, x, jnp.zeros_like(x))
            w = jnp.where(row < k_rem, w, jnp.zeros_like(w))
        return prod(x, w)

    def finish(y):
        if has_bias:
            y = y + b_ref[...]
        if epilogue is not None:
            y = epilogue(y)
        o_ref[...] = y.astype(o_ref.dtype)

    if nk == 1:
        finish(last_prod())
        return

    @pl.when(k == 0)
    def _first():
        acc_ref[...] = prod(x_ref[...], w_ref[...])

    if nk > 2:
        @pl.when((k > 0) & (k < nk - 1))
        def _middle():
            acc_ref[...] += prod(x_ref[...], w_ref[...])

    @pl.when(k == nk - 1)
    def _last():
        finish(acc_ref[...] + last_prod())


def _matmul(x, w3, li, *, tm, tn, tk, out_dtype=F32, bias=None, epilogue=None, name="mm"):
    m, kdim = x.shape
    _, kw, n = w3.shape
    assert kw == kdim and m % tm == 0 and n % tn == 0
    nk = pl.cdiv(kdim, tk)
    in_specs = [pl.BlockSpec((tm, tk), lambda i, j, k: (i, k)),
                pl.BlockSpec((None, tk, tn), lambda i, j, k: (li, k, j))]
    args = [x, w3]
    if bias is not None:
        in_specs.append(pl.BlockSpec((1, tn), lambda i, j, k: (0, j)))
        args.append(bias.reshape(1, n).astype(F32))
    return pl.pallas_call(
        functools.partial(_mm_body, nk=nk, k_rem=kdim % tk, epilogue=epilogue, has_bias=bias is not None),
        grid=(m // tm, n // tn, nk),
        in_specs=in_specs,
        out_specs=pl.BlockSpec((tm, tn), lambda i, j, k: (i, j)),
        out_shape=jax.ShapeDtypeStruct((m, n), out_dtype),
        scratch_shapes=[pltpu.VMEM((tm, tn), F32)],
        compiler_params=_cparams(("parallel", "parallel", "arbitrary")),
        name=name,
    )(*args)


def _mm_heads_body(x_ref, w_ref, cos_ref, sin_ref, o_ref, acc_ref, *, nk, n_rope, hd):
    j = pl.program_id(1)
    k = pl.program_id(2)
    assert nk >= 2

    def prod():
        return jnp.dot(x_ref[...].astype(BF16), w_ref[...].astype(BF16), preferred_element_type=F32)

    @pl.when(k == 0)
    def _first():
        acc_ref[...] = prod()

    if nk > 2:
        @pl.when((k > 0) & (k < nk - 1))
        def _middle():
            acc_ref[...] += prod()

    n_heads = acc_ref.shape[1] // hd

    @pl.when((k == nk - 1) & (j < n_rope))
    def _fin_rope():
        cos, sin = cos_ref[...], sin_ref[...]
        total = acc_ref[...] + prod()
        for h in range(n_heads):
            y = total[:, h * hd:(h + 1) * hd]
            o_ref[h] = y * cos + pltpu.roll(y, hd // 2, 1) * sin

    @pl.when((k == nk - 1) & (j >= n_rope))
    def _fin_plain():
        total = acc_ref[...] + prod()
        for h in range(n_heads):
            o_ref[h] = total[:, h * hd:(h + 1) * hd]


def _matmul_heads(x, w3, li, cos, sin_s, *, n_rope, tm, tn, tk, hd, name):
    m, kdim = x.shape
    _, _, n = w3.shape
    assert m % tm == 0 and n % tn == 0 and kdim % tk == 0 and tn % hd == 0
    nk = kdim // tk
    hpt = tn // hd
    return pl.pallas_call(
        functools.partial(_mm_heads_body, nk=nk, n_rope=n_rope, hd=hd),
        grid=(m // tm, n // tn, nk),
        in_specs=[pl.BlockSpec((tm, tk), lambda i, j, k: (i, k)),
                  pl.BlockSpec((None, tk, tn), lambda i, j, k: (li, k, j)),
                  pl.BlockSpec((tm, hd), lambda i, j, k: (i, 0)),
                  pl.BlockSpec((tm, hd), lambda i, j, k: (i, 0))],
        out_specs=pl.BlockSpec((hpt, tm, hd), lambda i, j, k: (j, i, 0)),
        out_shape=jax.ShapeDtypeStruct((n // hd, m, hd), F32),
        scratch_shapes=[pltpu.VMEM((tm, tn), F32)],
        compiler_params=_cparams(("parallel", "parallel", "arbitrary")),
        name=name,
    )(x, w3, cos, sin_s)


def _window_body(x_ref, o_ref, *, heads, hd):
    for h in range(heads):
        o_ref[:, h * hd:(h + 1) * hd] = x_ref[h]


def _kv_window(kv_hm, *, group, n_groups, heads, batch, seq, keep, name):
    hd = kv_hm.shape[-1]
    tmc = min(keep, 1024)
    assert keep % tmc == 0 and (seq - keep) % tmc == 0
    per_b = keep // tmc

    def in_map(i, c):
        b, t = i // per_b, i % per_b
        return (c * n_groups + group, (b * seq + seq - keep) // tmc + t, 0)

    return pl.pallas_call(
        functools.partial(_window_body, heads=heads, hd=hd),
        grid=(batch * per_b, 2),
        in_specs=[pl.BlockSpec((heads, tmc, hd), in_map)],
        out_specs=pl.BlockSpec((tmc, heads * hd), lambda i, c: (i, c)),
        out_shape=jax.ShapeDtypeStruct((batch * keep, 2 * heads * hd), F32),
        compiler_params=_cparams(("parallel", "parallel")),
        name=name,
    )(kv_hm)


def _swiglu_body(x_ref, wg_ref, wu_ref, o_ref, accg_ref, accu_ref, *, nk):
    k = pl.program_id(2)
    assert nk >= 2

    def prods():
        xb = x_ref[...].astype(BF16)
        return (jnp.dot(xb, wg_ref[...].astype(BF16), preferred_element_type=F32),
                jnp.dot(xb, wu_ref[...].astype(BF16), preferred_element_type=F32))

    @pl.when(k == 0)
    def _first():
        accg_ref[...], accu_ref[...] = prods()

    if nk > 2:
        @pl.when((k > 0) & (k < nk - 1))
        def _middle():
            pg, pu = prods()
            accg_ref[...] += pg
            accu_ref[...] += pu

    @pl.when(k == nk - 1)
    def _last():
        pg, pu = prods()
        g = accg_ref[...] + pg
        o_ref[...] = (g * jax.nn.sigmoid(g) * (accu_ref[...] + pu)).astype(o_ref.dtype)


def _swiglu_in(x, w3, li, *, tm, tn, tk, name="swiglu_in"):
    m, kdim = x.shape
    _, _, n2 = w3.shape
    hid = n2 // 2
    assert m % tm == 0 and hid % tn == 0 and kdim % tk == 0
    nk = kdim // tk
    nt = hid // tn
    return pl.pallas_call(
        functools.partial(_swiglu_body, nk=nk),
        grid=(m // tm, nt, nk),
        in_specs=[pl.BlockSpec((tm, tk), lambda i, j, k: (i, k)),
                  pl.BlockSpec((None, tk, tn), lambda i, j, k: (li, k, j)),
                  pl.BlockSpec((None, tk, tn), lambda i, j, k: (li, k, j + nt))],
        out_specs=pl.BlockSpec((tm, tn), lambda i, j, k: (i, j)),
        out_shape=jax.ShapeDtypeStruct((m, hid), BF16),
        scratch_shapes=[pltpu.VMEM((tm, tn), F32), pltpu.VMEM((tm, tn), F32)],
        compiler_params=_cparams(("parallel", "parallel", "arbitrary")),
        name=name,
    )(x, w3, w3)


def _ln_body(x_ref, f_ref, g_ref, b_ref, o_ref, ob_ref, *, alpha):
    z = alpha * x_ref[...] + f_ref[...]
    mu = jnp.mean(z, -1, keepdims=True)
    zc = z - mu
    var = jnp.mean(zc * zc, -1, keepdims=True)
    y = zc * lax.rsqrt(var + LN_EPS) * g_ref[...] + b_ref[...]
    o_ref[...] = y
    ob_ref[...] = y.astype(BF16)


def _res_ln(x, f, g, b, *, alpha, tm, name="res_ln"):
    m, d = x.shape
    assert m % tm == 0
    row = pl.BlockSpec((tm, d), lambda i: (i, 0))
    vec = pl.BlockSpec((1, d), lambda i: (0, 0))
    return pl.pallas_call(
        functools.partial(_ln_body, alpha=alpha),
        grid=(m // tm,),
        in_specs=[row, row, vec, vec],
        out_specs=[row, row],
        out_shape=[jax.ShapeDtypeStruct((m, d), F32), jax.ShapeDtypeStruct((m, d), BF16)],
        compiler_params=_cparams(("parallel",)),
        name=name,
    )(x, f, g.reshape(1, d), b.reshape(1, d))


def _shift_mix_body(x_ref, p_ref, m_ref, *o_refs):
    x = x_ref[...]
    xx = p_ref[...] - x
    for c, o_ref in enumerate(o_refs):
        o_ref[...] = (x + xx * m_ref[pl.ds(c, 1), :]).astype(BF16)


def _shift_mix(x, x_prev, mix, *, tm, name="shift_mix"):
    m, d = x.shape
    n_mix = mix.shape[0]
    assert m % tm == 0
    row = pl.BlockSpec((tm, d), lambda i: (i, 0))
    return pl.pallas_call(
        _shift_mix_body,
        grid=(m // tm,),
        in_specs=[row, row, pl.BlockSpec((n_mix, d), lambda i: (0, 0))],
        out_specs=[row] * n_mix,
        out_shape=[jax.ShapeDtypeStruct((m, d), BF16)] * n_mix,
        compiler_params=_cparams(("parallel",)),
        name=name,
    )(x, x_prev, mix.astype(F32))


def _combine_ln_body(x_ref, ya_ref, yb_ref, gt_ref, g_ref, b_ref, o_ref, *, alpha):
    gt = gt_ref[...]
    f = ya_ref[...] * gt[:, 0:1] + yb_ref[...] * gt[:, 1:2]
    z = alpha * x_ref[...] + f
    mu = jnp.mean(z, -1, keepdims=True)
    zc = z - mu
    var = jnp.mean(zc * zc, -1, keepdims=True)
    o_ref[...] = zc * lax.rsqrt(var + LN_EPS) * g_ref[...] + b_ref[...]


def _combine_ln(x, ya, yb, gates, g, b, *, alpha, tm, name="moe_combine_ln"):
    m, d = x.shape
    assert m % tm == 0
    row = pl.BlockSpec((tm, d), lambda i: (i, 0))
    vec = pl.BlockSpec((1, d), lambda i: (0, 0))
    return pl.pallas_call(
        functools.partial(_combine_ln_body, alpha=alpha),
        grid=(m // tm,),
        in_specs=[row, row, row, pl.BlockSpec((tm, gates.shape[1]), lambda i: (i, 0)), vec, vec],
        out_specs=row,
        out_shape=jax.ShapeDtypeStruct((m, d), F32),
        compiler_params=_cparams(("parallel",)),
        name=name,
    )(x, ya, yb, gates, g.reshape(1, d), b.reshape(1, d))


def _wkv_body(r_ref, w_ref, k_ref, v_ref, a_ref, b_ref, s0_ref, y_ref, s_ref, *, tc, n):
    t_blk = pl.program_id(1)

    @pl.when(t_blk == 0)
    def _init():
        s_ref[...] = s0_ref[...]

    def row(ref, t, j):
        return ref[t, pl.ds(j, 1), :]

    def step(t, carry):
        sa = jnp.zeros((n, LANES), F32)
        for j in range(n):
            sa = sa + s_ref[j] * row(a_ref, t, j)
        v_t = v_ref[t]
        y = jnp.zeros((n, LANES), F32)
        for j in range(n):
            s_new = s_ref[j] * row(w_ref, t, j) + sa * row(b_ref, t, j) + v_t * row(k_ref, t, j)
            s_ref[j] = s_new
            y = y + s_new * row(r_ref, t, j)
        y_ref[t] = y
        return carry

    lax.fori_loop(0, tc, step, 0)


def _wkv_scan(r, w, k, v, a, b, s0, *, tc, name="wkv_scan"):
    t_len, n, lanes = r.shape
    assert lanes % LANES == 0 and t_len % tc == 0
    seq = pl.BlockSpec((tc, n, LANES), lambda l, t: (t, 0, l))
    st = pl.BlockSpec((n, n, LANES), lambda l, t: (0, 0, l))
    return pl.pallas_call(
        functools.partial(_wkv_body, tc=tc, n=n),
        grid=(lanes // LANES, t_len // tc),
        in_specs=[seq] * 6 + [st],
        out_specs=[seq, st],
        out_shape=[jax.ShapeDtypeStruct((t_len, n, lanes), F32),
                   jax.ShapeDtypeStruct((n, n, lanes), F32)],
        compiler_params=_cparams(("parallel", "arbitrary")),
        name=name,
    )(r, w, k, v, a, b, s0)


CHUNK = 128
SUB = 32


def _head_normalize(x_t, hl, n):
    x3 = x_t.reshape(hl, n, CHUNK)
    nrm = jnp.sqrt(jnp.sum(x3 * x3, axis=1, keepdims=True))
    return (x3 / jnp.maximum(nrm, 1e-12)).reshape(hl * n, CHUNK)


def _to_scan_body(*refs, mode, nb, hl, n):
    n_in = {"copy": 1, "k2": 2, "kk": 1, "kb": 2}[mode]
    tiles = [refs[i * nb:(i + 1) * nb] for i in range(n_in)]
    pos = n_in * nb
    par_ref = None
    if mode != "copy":
        par_ref = refs[pos]
        pos += 1
    o_ref, sc_ref = refs[pos], refs[pos + 1]

    for b in range(nb):
        if mode == "copy":
            res = tiles[0][b][...].T
        elif mode == "k2":
            k, a = tiles[0][b][...], tiles[1][b][...]
            res = (k * (1.0 + (a - 1.0) * par_ref[...])).T
        elif mode == "kk":
            res = _head_normalize((tiles[0][b][...] * par_ref[...]).T, hl, n)
        else:
            kk = _head_normalize((tiles[0][b][...] * par_ref[...]).T, hl, n)
            res = kk * tiles[1][b][...].T
        sc_ref[b] = res

    for j in range(n):
        slab = jnp.concatenate([sc_ref.at[b][pl.ds(j, hl, stride=n), :] for b in range(nb)], axis=0)
        slab_t = slab.T
        for s in range(CHUNK // SUB):
            o_ref[s, pl.ds(j * SUB, SUB), :] = slab_t[s * SUB:(s + 1) * SUB, :]


def _to_scan(inputs, par, *, mode, nb, seq, heads, n, name):
    hl = LANES // nb
    width = hl * n
    n_hh = heads // hl
    n_c = seq // CHUNK
    in_specs, args = [], []
    for x in inputs:
        for b in range(nb):
            in_specs.append(pl.BlockSpec((CHUNK, width), lambda hh, c, b=b: (b * n_c + c, hh)))
            args.append(x)
    if par is not None:
        in_specs.append(pl.BlockSpec((1, width), lambda hh, c: (0, hh)))
        args.append(par.reshape(1, heads * n))
    return pl.pallas_call(
        functools.partial(_to_scan_body, mode=mode, nb=nb, hl=hl, n=n),
        grid=(n_hh, n_c),
        in_specs=in_specs,
        out_specs=pl.BlockSpec((CHUNK // SUB, n * SUB, LANES), lambda hh, c: (c, 0, hh)),
        out_shape=jax.ShapeDtypeStruct((seq // SUB, n * SUB, n_hh * LANES), F32),
        scratch_shapes=[pltpu.VMEM((nb, width, CHUNK), F32)],
        compiler_params=_cparams(("parallel", "parallel")),
        name=name,
    )(*args)


def _wkv_sl_body(r_ref, w_ref, k_ref, v_ref, kk_ref, b_ref, y_ref, s_ref, *, n):
    @pl.when(pl.program_id(1) == 0)
    def _init():
        s_ref[...] = jnp.zeros_like(s_ref)

    def row(ref, t, j):
        return ref[pl.ds(j * SUB + t, 1), :]

    def step(t, skk):
        t_next = jnp.minimum(t + 1, SUB - 1)
        v_t = v_ref[pl.ds(t, n, stride=SUB), :]
        y = jnp.zeros((n, LANES), F32)
        skk_next = jnp.zeros((n, LANES), F32)
        for j in range(n):
            s_new = s_ref[j] * row(w_ref, t, j) - skk * row(b_ref, t, j) + v_t * row(k_ref, t, j)
            s_ref[j] = s_new
            y = y + s_new * row(r_ref, t, j)
            skk_next = skk_next + s_new * row(kk_ref, t_next, j)
        y_ref[pl.ds(t, n, stride=SUB), :] = y
        return skk_next

    skk0 = jnp.zeros((n, LANES), F32)
    for j in range(n):
        skk0 = skk0 + s_ref[j] * row(kk_ref, 0, j)
    lax.fori_loop(0, SUB, step, skk0)


def _wkv_scan_sl(r, w, k, v, kk, b, *, n, name="wkv_scan_prompt"):
    n_blk, rows, lanes = r.shape
    seq = pl.BlockSpec((None, rows, LANES), lambda l, t: (t, 0, l))
    st = pl.BlockSpec((n, n, LANES), lambda l, t: (0, 0, l))
    return pl.pallas_call(
        functools.partial(_wkv_sl_body, n=n),
        grid=(lanes // LANES, n_blk),
        in_specs=[seq] * 6,
        out_specs=[seq, st],
        out_shape=[jax.ShapeDtypeStruct((n_blk, rows, lanes), F32),
                   jax.ShapeDtypeStruct((n, n, lanes), F32)],
        compiler_params=_cparams(("parallel", "arbitrary")),
        name=name,
    )(r, w, k, v, kk, b)


def _from_scan_body(y_ref, r_ref, k_ref, v_ref, rk_ref, lg_ref, lb_ref, g_ref, o_ref, z_ref, sc_ref,
                    *, nb, hl, n, eps):
    b = pl.program_id(2)

    @pl.when(b == 0)
    def _fill():
        for s in range(CHUNK // SUB):
            def blk(ref, j, s=s):
                return ref[s, pl.ds(j * SUB, SUB), :]

            coef = jnp.zeros((SUB, LANES), F32)
            tot = jnp.zeros((SUB, LANES), F32)
            for j in range(n):
                coef = coef + blk(r_ref, j) * blk(k_ref, j) * rk_ref[pl.ds(j, 1), :]
                tot = tot + blk(y_ref, j)
            mu = tot / n
            var = jnp.zeros((SUB, LANES), F32)
            for j in range(n):
                d = blk(y_ref, j) - mu
                var = var + d * d
            inv = lax.rsqrt(var / n + eps)
            for j in range(n):
                z = (blk(y_ref, j) - mu) * inv * lg_ref[pl.ds(j, 1), :] + lb_ref[pl.ds(j, 1), :]
                z_ref[j, pl.ds(s * SUB, SUB), :] = z + coef * blk(v_ref, j)
        for j in range(n):
            slab_t = z_ref[j].T
            for bb in range(nb):
                sc_ref.at[bb][pl.ds(j, hl, stride=n), :] = slab_t[bb * hl:(bb + 1) * hl, :]

    o_ref[...] = (sc_ref[b].T * g_ref[...]).astype(o_ref.dtype)


def _from_scan(y, r, k2, v, rk_t, lg_t, lb_t, g, *, nb, seq, heads, n, eps, name="rwkv_post"):
    hl = LANES // nb
    width = hl * n
    n_hh = heads // hl
    n_c = seq // CHUNK
    sl = pl.BlockSpec((CHUNK // SUB, n * SUB, LANES), lambda hh, c, b: (c, 0, hh))
    par = pl.BlockSpec((n, LANES), lambda hh, c, b: (0, hh))
    tok = pl.BlockSpec((CHUNK, width), lambda hh, c, b: (b * n_c + c, hh))
    return pl.pallas_call(
        functools.partial(_from_scan_body, nb=nb, hl=hl, n=n, eps=eps),
        grid=(n_hh, n_c, nb),
        in_specs=[sl, sl, sl, sl, par, par, par, tok],
        out_specs=tok,
        out_shape=jax.ShapeDtypeStruct((nb * seq, heads * n), BF16),
        scratch_shapes=[pltpu.VMEM((n, CHUNK, LANES), F32), pltpu.VMEM((nb, width, CHUNK), F32)],
        compiler_params=_cparams(("parallel", "parallel", "arbitrary")),
        name=name,
    )(y, r, k2, v, rk_t, lg_t, lb_t, g)


def _head_param_lanes(p, nb, heads, n):
    hl = LANES // nb
    t = p.reshape(heads // hl, 1, hl, n)
    t = jnp.broadcast_to(t, (heads // hl, nb, hl, n))
    return jnp.transpose(t, (3, 0, 1, 2)).reshape(n, (heads // hl) * LANES).astype(F32)


SPAN = 128
ATTN_UNROLL = 5


def _attn_block(q, k, v, causal_own, scale):
    s = lax.dot_general(q.astype(BF16), k.astype(BF16), (((1,), (1,)), ((), ())),
                        preferred_element_type=F32) * scale
    qi = lax.broadcasted_iota(jnp.int32, s.shape, 0)
    ki = lax.broadcasted_iota(jnp.int32, s.shape, 1)
    if causal_own:
        mask = ki <= qi
    else:
        mask = (ki >= qi) & (ki <= qi + SPAN)
    s = jnp.where(mask, s, -jnp.inf)
    m = jnp.max(s, axis=1, keepdims=True)
    p = jnp.exp(s - m)
    l = jnp.sum(p, axis=1, keepdims=True)
    o = jnp.dot(p.astype(BF16), v.astype(BF16), preferred_element_type=F32)
    return o, m, l


def _dilated_attn_body(*refs, dils, seq, scale):
    ng = len(dils)
    q_refs, k_refs, v_refs = refs[:ng], refs[ng:2 * ng], refs[2 * ng:3 * ng]
    o_ref = refs[3 * ng]
    scratch = refs[3 * ng + 1:]
    acc_refs, m_refs, l_refs = scratch[:ng], scratch[ng:2 * ng], scratch[2 * ng:]

    for g, dil in enumerate(dils):
        q_ref, k_ref, v_ref = q_refs[g], k_refs[g], v_refs[g]
        acc_ref, m_ref, l_ref = acc_refs[g], m_refs[g], l_refs[g]
        nblk = seq // (dil * SPAN)

        def rows(start, size, dil=dil):
            if dil == 1:
                return pl.ds(start, size)
            return pl.ds(start, size, stride=dil)

        def put(sl, res, acc_ref=acc_ref, m_ref=m_ref, l_ref=l_ref):
            o, m, l = res
            acc_ref[sl, :] = o
            m_ref[sl, :] = m
            l_ref[sl, :] = l

        for r in range(dil):
            sl0 = rows(r, SPAN)
            put(sl0, _attn_block(q_ref[sl0, :], k_ref[sl0, :], v_ref[sl0, :], True, scale))

            def body(nb, carry, r=r, dil=dil, rows=rows, put=put, q_ref=q_ref, k_ref=k_ref, v_ref=v_ref):
                q_sl = rows(r + dil * SPAN * nb, SPAN)
                kv_sl = rows(r + dil * SPAN * (nb - 1), 2 * SPAN)
                put(q_sl, _attn_block(q_ref[q_sl, :], k_ref[kv_sl, :], v_ref[kv_sl, :], False, scale))
                return carry

            if nblk > 1:
                lax.fori_loop(1, nblk, body, 0, unroll=ATTN_UNROLL)

    m_all = m_refs[0][...]
    for g in range(1, ng):
        m_all = jnp.maximum(m_all, m_refs[g][...])
    num = jnp.zeros(acc_refs[0].shape, F32)
    den = jnp.zeros(m_all.shape, F32)
    for g in range(ng):
        e = jnp.exp(m_refs[g][...] - m_all)
        num = num + e * acc_refs[g][...]
        den = den + e * l_refs[g][...]
    o_ref[...] = (num / den).astype(o_ref.dtype)


def _dilated_attn(q_hm, k_hm, v_hm, *, batch, seq, heads, dils, v_head0=0, name="dilated_attn"):
    e = q_hm.shape[-1]
    ng = len(dils)

    def spec(g, h0=0):
        return pl.BlockSpec((None, seq, e), lambda b, h, g=g: (h0 + g * heads + h, b, 0))

    specs = [spec(g) for g in range(ng)]
    return pl.pallas_call(
        functools.partial(_dilated_attn_body, dils=dils, seq=seq, scale=e ** -0.5),
        grid=(batch, heads),
        in_specs=specs * 2 + [spec(g, v_head0) for g in range(ng)],
        out_specs=pl.BlockSpec((seq, e), lambda b, h: (b, h)),
        out_shape=jax.ShapeDtypeStruct((batch * seq, heads * e), BF16),
        scratch_shapes=([pltpu.VMEM((seq, e), F32)] * ng + [pltpu.VMEM((seq, 1), F32)] * (2 * ng)),
        compiler_params=_cparams(("parallel", "parallel")),
        name=name,
    )(*([q_hm] * ng + [k_hm] * ng + [v_hm] * ng))


def _sample_attn_body(q_ref, kn_ref, vn_ref, *rest, ng, heads, scale):
    cache_refs, o_ref = rest[:ng], rest[ng]
    for h in range(heads):
        parts = []
        for g in range(ng):
            row = g * heads + h
            q = q_ref[pl.ds(row, 1), :]
            kc = cache_refs[g][:, 0, h, :]
            vc = cache_refs[g][:, 1, h, :]
            kn, vn = kn_ref[pl.ds(row, 1), :], vn_ref[pl.ds(row, 1), :]
            qb = q.astype(BF16)
            s = lax.dot_general(qb, kc.astype(BF16), (((1,), (1,)), ((), ())),
                                preferred_element_type=F32) * scale
            s_n = jnp.sum(qb.astype(F32) * kn.astype(BF16).astype(F32), axis=1, keepdims=True) * scale
            m = jnp.maximum(jnp.max(s, axis=1, keepdims=True), s_n)
            p, p_n = jnp.exp(s - m), jnp.exp(s_n - m)
            l = jnp.sum(p, axis=1, keepdims=True) + p_n
            o = jnp.dot(p.astype(BF16), vc.astype(BF16), preferred_element_type=F32)
            o = o + p_n.astype(BF16).astype(F32) * vn.astype(BF16).astype(F32)
            parts.append((o, m, l))
        m_all = parts[0][1]
        for _, m, _ in parts[1:]:
            m_all = jnp.maximum(m_all, m)
        num = jnp.zeros_like(parts[0][0])
        den = jnp.zeros_like(m_all)
        for o, m, l in parts:
            e = jnp.exp(m - m_all)
            num = num + e * o
            den = den + e * l
        o_ref[pl.ds(h, 1), :] = (num / den).astype(o_ref.dtype)


def _sample_attn(q_s, k_new, v_new, caches, dils, *, heads, name="sample_attn"):
    bs, _, e = q_s.shape
    ng = len(dils)
    tok = pl.BlockSpec((None, ng * heads, e), lambda b: (b, 0, 0))
    in_specs, args = [tok, tok, tok], [q_s, k_new, v_new]
    for c, dil in zip(caches, dils):
        w = c.shape[1]
        assert w == SPAN * dil
        args.append(c.reshape(bs, SPAN, dil, 2, heads, e))
        in_specs.append(pl.BlockSpec((None, SPAN, None, 2, heads, e), lambda b: (b, 0, 0, 0, 0, 0)))
    return pl.pallas_call(
        functools.partial(_sample_attn_body, ng=ng, heads=heads, scale=e ** -0.5),
        grid=(bs,),
        in_specs=in_specs,
        out_specs=pl.BlockSpec((None, heads, e), lambda b: (b, 0, 0)),
        out_shape=jax.ShapeDtypeStruct((bs, heads, e), BF16),
        compiler_params=_cparams(("parallel",)),
        name=name,
    )(*args)


def _router_body(x_ref, w_ref, o_ref):
    o_ref[...] = jnp.dot(x_ref[...], w_ref[...], precision=lax.Precision.HIGHEST,
                         preferred_element_type=F32)


def _router_logits(x, router, *, tm, name="router_logits"):
    m, d = x.shape
    n_exp = router.shape[1]
    w = jnp.zeros((d, LANES), F32).at[:, :n_exp].set(router.astype(F32))
    out = pl.pallas_call(
        _router_body,
        grid=(m // tm,),
        in_specs=[pl.BlockSpec((tm, d), lambda i: (i, 0)), pl.BlockSpec((d, LANES), lambda i: (0, 0))],
        out_specs=pl.BlockSpec((tm, LANES), lambda i: (i, 0)),
        out_shape=jax.ShapeDtypeStruct((m, LANES), F32),
        compiler_params=_cparams(("parallel",)),
        name=name,
    )(x, w)
    return out[:, :n_exp]


def _moe_in_body(be_ref, ok_ref, src_ref, x_ref, wg_ref, wu_ref, o_ref, wgb_ref, wub_ref):
    r = pl.program_id(1)
    changed = (r == 0) | (be_ref[r] != be_ref[jnp.maximum(r - 1, 0)])

    @pl.when(changed)
    def _cast():
        wgb_ref[...] = wg_ref[...].astype(BF16)
        wub_ref[...] = wu_ref[...].astype(BF16)

    @pl.when(ok_ref[r] != 0)
    def _compute():
        x = x_ref[...]
        g = jnp.dot(x, wgb_ref[...], preferred_element_type=F32)
        u = jnp.dot(x, wub_ref[...], preferred_element_type=F32)
        o_ref[...] = (g * jax.nn.sigmoid(g) * u).astype(o_ref.dtype)

    @pl.when(ok_ref[r] == 0)
    def _skip():
        o_ref[...] = jnp.zeros_like(o_ref)


def _moe_in(x_buf, w_in, blk_e, blk_ok, blk_src, *, tn, name="moe_swiglu_in"):
    rows, d = x_buf.shape
    n_exp, _, n2 = w_in.shape
    hid = n2 // 2
    nt = hid // tn
    rb = rows // MOE_ROWS
    grid_spec = pltpu.PrefetchScalarGridSpec(
        num_scalar_prefetch=3,
        grid=(nt, rb),
        in_specs=[pl.BlockSpec((MOE_ROWS, d), lambda j, r, be, ok, src: (src[r], 0)),
                  pl.BlockSpec((None, d, tn), lambda j, r, be, ok, src: (be[r], 0, j)),
                  pl.BlockSpec((None, d, tn), lambda j, r, be, ok, src: (be[r], 0, j + nt))],
        out_specs=pl.BlockSpec((MOE_ROWS, tn), lambda j, r, be, ok, src: (r, j)),
        scratch_shapes=[pltpu.VMEM((d, tn), BF16), pltpu.VMEM((d, tn), BF16)],
    )
    return pl.pallas_call(
        _moe_in_body,
        grid_spec=grid_spec,
        out_shape=jax.ShapeDtypeStruct((rows, hid), BF16),
        compiler_params=_cparams(("arbitrary", "arbitrary")),
        name=name,
    )(blk_e, blk_ok, blk_src, x_buf, w_in, w_in)


def _moe_out_body(pe_ref, nv_ref, src_ref, h_ref, w_ref, o_ref, acc_ref, *, nk):
    p = pl.program_id(0)
    k = pl.program_id(2)
    nv = nv_ref[p]

    @pl.when((k == 0) & (nv != 2))
    def _init():
        acc_ref[...] = jnp.zeros_like(acc_ref)

    @pl.when((nv == 2) & (k == 0))
    def _both_first():
        acc_ref[...] = jnp.dot(h_ref[...], w_ref[...].astype(BF16), preferred_element_type=F32)

    @pl.when((nv == 2) & (k > 0))
    def _both():
        acc_ref[...] += jnp.dot(h_ref[...], w_ref[...].astype(BF16), preferred_element_type=F32)

    @pl.when(nv == 1)
    def _first():
        acc_ref[:MOE_ROWS, :] += jnp.dot(h_ref[:MOE_ROWS, :], w_ref[...].astype(BF16),
                                         preferred_element_type=F32)

    @pl.when(k == nk - 1)
    def _fin():
        o_ref[...] = acc_ref[...]


def _moe_out(h_buf, w_out, pair_e, pair_nv, pair_src, *, tn, tk, name="moe_out"):
    rows, hid = h_buf.shape
    n_exp, _, d = w_out.shape
    n_pair = rows // (2 * MOE_ROWS)
    nk = hid // tk

    def k_eff(p, k, nv):
        return jnp.where(nv[p] > 0, k, nk - 1)

    grid_spec = pltpu.PrefetchScalarGridSpec(
        num_scalar_prefetch=3,
        grid=(n_pair, d // tn, nk),
        in_specs=[pl.BlockSpec((2 * MOE_ROWS, tk), lambda p, j, k, pe, nv, src: (src[p], k_eff(p, k, nv))),
                  pl.BlockSpec((None, tk, tn), lambda p, j, k, pe, nv, src: (pe[p], k_eff(p, k, nv), j))],
        out_specs=pl.BlockSpec((2 * MOE_ROWS, tn), lambda p, j, k, pe, nv, src: (p, j)),
        scratch_shapes=[pltpu.VMEM((2 * MOE_ROWS, tn), F32)],
    )
    return pl.pallas_call(
        functools.partial(_moe_out_body, nk=nk),
        grid_spec=grid_spec,
        out_shape=jax.ShapeDtypeStruct((rows, d), F32),
        compiler_params=_cparams(("arbitrary", "arbitrary", "arbitrary")),
        name=name,
    )(pair_e, pair_nv, pair_src, h_buf, w_out)


def _moe(x_f32, n_tok, router, w_in, w_out):
    rows, d = x_f32.shape
    n_exp = router.shape[1]
    top_k = 2
    logits = _router_logits(x_f32, router, tm=520)[:n_tok]
    top_v, top_i = lax.top_k(logits, top_k)
    gates = jax.nn.softmax(top_v, axis=-1)
    n_asg = n_tok * top_k
    e = top_i.reshape(n_asg).astype(jnp.int32)
    tok = jnp.repeat(jnp.arange(n_tok, dtype=jnp.int32), top_k)
    order = jnp.argsort(e)
    e_s, tok_s = e[order], tok[order]
    counts = jnp.bincount(e, length=n_exp).astype(jnp.int32)
    starts = jnp.cumsum(counts) - counts
    padded = -(-counts // MOE_ROWS) * MOE_ROWS
    region = -(-padded // (2 * MOE_ROWS)) * (2 * MOE_ROWS)
    region_ends = jnp.cumsum(region)
    region_starts = region_ends - region
    dest = (region_starts[e_s] + jnp.arange(n_asg, dtype=jnp.int32) - starts[e_s]).astype(jnp.int32)
    n_pair = (-(-n_asg // MOE_ROWS) + 2 * n_exp + 1) // 2
    n_blk = 2 * n_pair
    src_tok = (jnp.arange(n_blk * MOE_ROWS, dtype=jnp.int32) % n_tok).at[dest].set(tok_s)
    pos = jnp.zeros((n_asg,), jnp.int32).at[order].set(dest).reshape(n_tok, top_k)
    blk_start = jnp.arange(n_blk, dtype=jnp.int32) * MOE_ROWS
    blk_e = jnp.minimum(jnp.searchsorted(region_ends, blk_start, side='right'), n_exp - 1).astype(jnp.int32)
    blk_ok = ((blk_start - region_starts[blk_e]) < padded[blk_e]).astype(jnp.int32)
    blk_src = lax.cummax(jnp.where(blk_ok > 0, jnp.arange(n_blk, dtype=jnp.int32), 0))
    pair_nv = blk_ok.reshape(n_pair, 2).sum(axis=1).astype(jnp.int32)
    pair_src = lax.cummax(jnp.where(pair_nv > 0, jnp.arange(n_pair, dtype=jnp.int32), 0))

    blk_ids = jnp.arange(n_blk, dtype=jnp.int32)
    blk_next = lax.cummin(jnp.where(blk_ok > 0, blk_ids, n_blk - 1), axis=0, reverse=True)
    x_buf = x_f32[src_tok].astype(BF16)
    h_buf = _moe_in(x_buf, w_in, blk_e[blk_next], blk_ok, blk_src, tn=512)
    y_buf = _moe_out(h_buf, w_out, blk_e[::2], pair_nv, pair_src, tn=min(2048, d), tk=1024)
    pos = jnp.zeros((rows, top_k), jnp.int32).at[:n_tok].set(pos)
    gates = jnp.zeros((rows, top_k), F32).at[:n_tok].set(gates)
    return y_buf[pos[:, 0]], y_buf[pos[:, 1]], gates


def _rope_tables(pos, hd):
    inv = ROPE_THETA ** (-jnp.arange(0, hd, 2, dtype=F32) / hd)
    ang = pos.astype(F32)[:, None] * inv[None, :]
    ang = jnp.concatenate([ang, ang], -1)
    return jnp.cos(ang), jnp.sin(ang)


def _softplus(z):
    return jnp.maximum(z, 0.0) + jnp.log(1.0 + jnp.exp(-jnp.abs(z)))


def _decay_epilogue(y):
    return jnp.exp(-jnp.exp(-_softplus(-y) - 0.5))


def kernel(x_prompt, x_sample, state_wkv, state_shift, cache_kv_w128, cache_kv_w512, cache_kv_w2048,
           ln_g, ln_b, rwkv_mix, rwkv_w_rkv, rwkv_w0, rwkv_w1, rwkv_w2, rwkv_a0, rwkv_a1, rwkv_a2,
           rwkv_g1, rwkv_g2, rwkv_k_k, rwkv_k_a, rwkv_r_k, rwkv_lnx_g, rwkv_lnx_b, rwkv_w_out,
           attn_w_kv, attn_w_q, attn_w_out, ffn_w_in, ffn_w_out, moe_router, moe_w_in, moe_w_out):
    bp, seq, d = x_prompt.shape
    bs, seq_s, _ = x_sample.shape
    depth = ln_g.shape[0]
    assert depth == 2 and seq_s == 1
    heads_r, hd_r = rwkv_r_k.shape[1], rwkv_r_k.shape[2]
    past_len = PAST_LEN
    groups = ((128, 1), (512, 4), (2048, 16))
    ng = len(groups)
    hd_a = 128
    heads_a = attn_w_out.shape[1] // hd_a
    alpha = (2.0 * depth) ** 0.25
    lnx_eps = 1e-5 * hd_r

    n_p = bp * seq
    n_tok = n_p + bs
    tm = 2080
    rows = -(-n_tok // tm) * tm

    def pad_rows(a):
        return jnp.concatenate([a, jnp.zeros((rows - a.shape[0],) + a.shape[1:], a.dtype)], 0)

    x0 = pad_rows(jnp.concatenate([x_prompt.reshape(n_p, d), x_sample.reshape(bs, d)], 0))
    prev_p = jnp.concatenate([jnp.zeros((bp, 1, d), F32), x_prompt[:, :-1]], 1).reshape(n_p, d)
    x_prev = pad_rows(jnp.concatenate([prev_p, state_shift[0]], 0))
    xs = _shift_mix(x0, x_prev, rwkv_mix[0], tm=208)

    w_rkv = rwkv_w_rkv.reshape(3, d, d)
    mm = functools.partial(_matmul, tm=tm)
    r = mm(xs[0], w_rkv, 0, tn=1024, tk=1024, name="rwkv_r")
    k = mm(xs[1], w_rkv, 1, tn=1024, tk=1024, name="rwkv_k")
    v = mm(xs[2], w_rkv, 2, tn=1024, tk=1024, name="rwkv_v")
    lw, la, lg = rwkv_w1.shape[2], rwkv_a1.shape[2], rwkv_g1.shape[2]
    hw = mm(xs[3], rwkv_w1, 0, tn=lw, tk=1024, out_dtype=BF16, epilogue=jnp.tanh, name="rwkv_w1")
    decay = mm(hw, rwkv_w2, 0, tn=1024, tk=lw, bias=rwkv_w0[0], epilogue=_decay_epilogue, name="rwkv_w2")
    ha = mm(xs[4], rwkv_a1, 0, tn=la, tk=1024, out_dtype=BF16, name="rwkv_a1")
    a = mm(ha, rwkv_a2, 0, tn=1024, tk=la, bias=rwkv_a0[0], epilogue=jax.nn.sigmoid, name="rwkv_a2")
    hg = mm(xs[5], rwkv_g1, 0, tn=lg, tk=1024, out_dtype=BF16, epilogue=jax.nn.sigmoid, name="rwkv_g1")
    g = mm(hg, rwkv_g2, 0, tn=1024, tk=lg, name="rwkv_g2")

    hl = LANES // bp
    assert LANES % bp == 0 and heads_r % hl == 0 and seq % CHUNK == 0
    kw = dict(nb=bp, seq=seq, heads=heads_r, n=hd_r)
    r_sl = _to_scan([r], None, mode="copy", name="sl_r", **kw)
    w_sl = _to_scan([decay], None, mode="copy", name="sl_w", **kw)
    v_sl = _to_scan([v], None, mode="copy", name="sl_v", **kw)
    k_sl = _to_scan([k, a], rwkv_k_a[0], mode="k2", name="sl_k", **kw)
    kk_sl = _to_scan([k], rwkv_k_k[0], mode="kk", name="sl_kk", **kw)
    b_sl = _to_scan([k, a], rwkv_k_k[0], mode="kb", name="sl_b", **kw)
    y_sl, s_p = _wkv_scan_sl(r_sl, w_sl, k_sl, v_sl, kk_sl, b_sl, n=hd_r)
    tab = functools.partial(_head_param_lanes, nb=bp, heads=heads_r, n=hd_r)
    y_p = _from_scan(y_sl, r_sl, k_sl, v_sl, tab(rwkv_r_k.reshape(-1)), tab(rwkv_lnx_g[0]),
                     tab(rwkv_lnx_b[0]), g, eps=lnx_eps, **kw)
    s_p = s_p.reshape(hd_r, hd_r, heads_r // hl, bp, hl)
    prompt_wkv = jnp.transpose(s_p, (3, 2, 4, 1, 0)).reshape(1, bp, heads_r, hd_r, hd_r).astype(state_wkv.dtype)

    def hs(t):
        return t[n_p:n_tok].reshape(bs, heads_r, hd_r)

    kk_s = hs(k) * rwkv_k_k[0].reshape(heads_r, hd_r)
    kk_s = kk_s / jnp.maximum(jnp.linalg.norm(kk_s, axis=-1, keepdims=True), 1e-12)
    k2_s = hs(k) * (1.0 + (hs(a) - 1.0) * rwkv_k_a[0].reshape(heads_r, hd_r))
    bonus_s = jnp.sum(hs(r) * k2_s * rwkv_r_k[0], -1, keepdims=True) * hs(v)

    def lanes_s(t):
        return jnp.transpose(t, (2, 0, 1)).reshape(1, hd_r, bs * heads_r)

    scan_s = (hs(r), hs(decay), k2_s, hs(v), -kk_s, kk_s * hs(a))
    s0_s = jnp.transpose(state_wkv[0].astype(F32), (3, 2, 0, 1)).reshape(hd_r, hd_r, bs * heads_r)
    y_s, s_s = _wkv_scan(*[lanes_s(t) for t in scan_s], s0_s, tc=1, name="wkv_scan_sample")
    y_s = jnp.transpose(y_s.reshape(hd_r, bs, heads_r), (1, 2, 0))
    sample_wkv = jnp.transpose(s_s.reshape(hd_r, hd_r, bs, heads_r), (2, 3, 1, 0))[None].astype(state_wkv.dtype)
    mu = jnp.mean(y_s, -1, keepdims=True)
    var = jnp.mean(jnp.square(y_s - mu), -1, keepdims=True)
    y_s = ((y_s - mu) * lax.rsqrt(var + lnx_eps)).reshape(bs, d) * rwkv_lnx_g[0] + rwkv_lnx_b[0]
    y_s = ((y_s + bonus_s.reshape(bs, d)) * g[n_p:n_tok]).astype(BF16)

    y = pad_rows(jnp.concatenate([y_p, y_s], 0))
    mix0 = mm(y, rwkv_w_out, 0, tn=1024, tk=1024, name="rwkv_out")
    x1, x1b = _res_ln(x0, mix0, ln_g[0, 0], ln_b[0, 0], alpha=alpha, tm=208, name="ln_0a")

    h = _swiglu_in(x1b, ffn_w_in, 0, tm=tm, tn=256, tk=2048, name="ffn_in")
    f0 = mm(h, ffn_w_out, 0, tn=1024, tk=1024, name="ffn_out")
    x2, x2b = _res_ln(x1, f0, ln_g[0, 1], ln_b[0, 1], alpha=alpha, tm=208, name="ln_0b")

    pos = jnp.concatenate([jnp.tile(jnp.arange(seq, dtype=jnp.int32), bp),
                           jnp.full((bs,), past_len, jnp.int32),
                           jnp.zeros((rows - n_tok,), jnp.int32)])
    cos, sin = _rope_tables(pos, hd_a)
    half_sign = jnp.concatenate([-jnp.ones((hd_a // 2,), F32), jnp.ones((hd_a // 2,), F32)])
    sin_s = sin * half_sign
    n_qh = ng * heads_a
    tn_h = 1024
    kv_hm = _matmul_heads(x2b, attn_w_kv.reshape(1, d, -1), 0, cos, sin_s, n_rope=n_qh * hd_a // tn_h,
                          tm=tm, tn=tn_h, tk=1024, hd=hd_a, name="attn_kv")
    q_hm = _matmul_heads(x2b, attn_w_q, 0, cos, sin_s, n_rope=n_qh * hd_a // tn_h,
                         tm=tm, tn=tn_h, tk=1024, hd=hd_a, name="attn_q")
    dils = tuple(dl for _, dl in groups)
    att_p = _dilated_attn(q_hm, kv_hm, kv_hm, batch=bp, seq=seq, heads=heads_a, dils=dils, v_head0=n_qh)

    caches = (cache_kv_w128, cache_kv_w512, cache_kv_w2048)
    q_s = jnp.transpose(q_hm[:, n_p:n_tok], (1, 0, 2))
    kv_s = jnp.transpose(kv_hm[:, n_p:n_tok], (1, 0, 2)).reshape(bs, 2, n_qh, hd_a)
    att_s = _sample_attn(q_s, kv_s[:, 0], kv_s[:, 1], caches, dils, heads=heads_a)
    att = pad_rows(jnp.concatenate([att_p, att_s.reshape(bs, heads_a * hd_a)], 0))
    mix1 = mm(att, attn_w_out, 0, tn=1024, tk=1024, name="attn_out")
    x3, x3b = _res_ln(x2, mix1, ln_g[1, 0], ln_b[1, 0], alpha=alpha, tm=208, name="ln_1a")

    n_exp = moe_router.shape[-1]
    ya, yb, gates = _moe(x3, n_tok, moe_router.reshape(d, n_exp), moe_w_in.reshape(n_exp, d, -1),
                         moe_w_out.reshape(n_exp, -1, d))
    x4 = _combine_ln(x3, ya, yb, gates, ln_g[1, 1], ln_b[1, 1], alpha=alpha, tm=208)

    y_prompt = x4[:n_p].reshape(bp, seq, d)
    y_sample = x4[n_p:n_tok].reshape(bs, 1, d)
    prompt_shift = x_prompt[:, -1][None]
    sample_shift = x_sample[:, -1][None]
    prompt_kv, sample_kv = [], []
    kv_n = kv_s.reshape(bs, 1, 2, ng, heads_a, hd_a)
    for gi, (win, dil) in enumerate(groups):
        keep = min(win, seq)
        rows_g = _kv_window(kv_hm, group=gi, n_groups=ng, heads=heads_a, batch=bp, seq=seq, keep=keep,
                            name="kv_window_%d" % win)
        prompt_kv.append(rows_g.reshape(bp, keep, 2, heads_a, hd_a))
        sample_kv.append(kv_n[:, :, :, gi])
    return (y_prompt, y_sample, prompt_wkv, prompt_shift, prompt_kv[0], prompt_kv[1], prompt_kv[2],
            sample_wkv, sample_shift, sample_kv[0], sample_kv[1], sample_kv[2])
```
